```python
import jax, jax.numpy as jnp
from jax import lax
import numpy as np

D_MODEL = 1024
BATCH = 16
SEQ = 4096
DEPTH = 1

PLE_DIM = 256
HEAD_DIM = 64
HALF_DIM = HEAD_DIM // 2
ROPE_THETA = 10000.0
BLOCK_Q = 128
LN_EPS = 1e-5
NEG_INF = -1e30
FORCE = 1e30
TINY = 1e-30

SWA_HEADS = 8
SWA_KV_HEADS = 1
SWA_WINDOW = 128

NSA_HEADS = 8
NSA_GROUPS = 2
NSA_WINDOW = 512
CMP_BLOCK = 32
CMP_STRIDE = 16
CMP_HIDDEN = 256
SEL_BLOCK = 64
N_SEL = 16
SEL_CHUNK = 64
N_NSA_BRANCH = 3

N_BRANCH = 2
D_FF = -(-8 * D_MODEL // (3 * 256)) * 256

SWA_Q = SWA_HEADS * HEAD_DIM
SWA_KV = SWA_KV_HEADS * HEAD_DIM
NSA_Q = NSA_HEADS * HEAD_DIM
NSA_KV = NSA_GROUPS * HEAD_DIM
NSA_GATES = NSA_HEADS * N_NSA_BRANCH
IN_SPLITS = (SWA_Q, SWA_KV, SWA_KV, NSA_Q, NSA_KV, NSA_KV, NSA_KV, NSA_KV, NSA_KV, NSA_KV, NSA_GATES, N_BRANCH * D_MODEL)
IN_IS_VALUE = (False, False, True, False, False, True, False, True, False, True, False, False)
D_IN = SWA_Q + 2 * SWA_KV + NSA_Q + 6 * NSA_KV + NSA_GATES + N_BRANCH * D_MODEL

kernel_name = 'hybrid_swa_sink_nsa_deepnorm_block'


def layer_norm(x, g, b):
    xf = x.astype(jnp.float32)
    mu = jnp.mean(xf, axis=-1, keepdims=True)
    var = jnp.mean(jnp.square(xf - mu), axis=-1, keepdims=True)
    return ((xf - mu) * lax.rsqrt(var + LN_EPS) * g + b).astype(x.dtype)


def rope_tables(positions, dtype):
    inv = ROPE_THETA ** (-jnp.arange(0, HEAD_DIM, 2, dtype=jnp.float32) / HEAD_DIM)
    ang = positions.astype(jnp.float32)[..., None] * inv
    return jnp.cos(ang)[:, :, None, :].astype(dtype), jnp.sin(ang)[:, :, None, :].astype(dtype)


def apply_rope(x, cos, sin):
    x1, x2 = x[..., :HALF_DIM], x[..., HALF_DIM:]
    return jnp.concatenate([x1 * cos - x2 * sin, x2 * cos + x1 * sin], axis=-1)


def masked_softmax(s, mask, sink=None):
    s = jnp.where(mask, s, NEG_INF)
    m = jnp.max(s, axis=-1, keepdims=True)
    if sink is not None:
        m = jnp.maximum(m, sink)
    e = jnp.where(mask, jnp.exp(s - m), 0.0)
    den = jnp.sum(e, axis=-1, keepdims=True)
    if sink is not None:
        den = den + jnp.exp(sink - m)
    return e / jnp.maximum(den, TINY)


def split_in(z):
    outs = []
    off = 0
    for n in IN_SPLITS:
        outs.append(z[..., off:off + n])
        off += n
    return outs


def banded_attention(q, k, v, window, sink=None):
    B, S, G, R, dh = q.shape
    n_prev = -(-window // BLOCK_Q)
    pad = n_prev * BLOCK_Q
    band = pad + BLOCK_Q
    kp = jnp.pad(k, ((0, 0), (pad, 0), (0, 0), (0, 0)))
    vp = jnp.pad(v, ((0, 0), (pad, 0), (0, 0), (0, 0)))
    scale = dh ** -0.5

    def one_block(c):
        start = c * BLOCK_Q
        qb = lax.dynamic_slice_in_dim(q, start, BLOCK_Q, axis=1)
        kb = lax.dynamic_slice_in_dim(kp, start, band, axis=1)
        vb = lax.dynamic_slice_in_dim(vp, start, band, axis=1)
        s = jnp.einsum('bqgrd,bkgd->bgrqk', qb, kb, preferred_element_type=jnp.float32) * scale
        qpos = start + jnp.arange(BLOCK_Q)
        kpos = start - pad + jnp.arange(band)
        diff = qpos[:, None] - kpos[None, :]
        mask = (diff >= 0) & (diff < window) & (kpos[None, :] >= 0)
        pr = masked_softmax(s, mask, sink)
        return jnp.einsum('bgrqk,bkgd->bqgrd', pr.astype(v.dtype), vb)

    out = lax.map(one_block, jnp.arange(S // BLOCK_Q))
    return out.transpose(1, 0, 2, 3, 4, 5).reshape(B, S, G, R, dh)


def compress_tokens(kv, pos_emb, w1, w2):
    B, S, G, dh = kv.shape
    n_sub = CMP_BLOCK // CMP_STRIDE
    chunks = kv.reshape(B, S // CMP_STRIDE, CMP_STRIDE, G, dh)
    n_cmp = S // CMP_STRIDE - n_sub + 1
    blocks = jnp.concatenate([chunks[:, j:j + n_cmp] for j in range(n_sub)], axis=2)
    blocks = blocks + pos_emb[:, None, :]
    flat = blocks.transpose(0, 1, 3, 2, 4).reshape(B, n_cmp, G, CMP_BLOCK * dh)
    return jax.nn.gelu(flat @ w1) @ w2


def selected_attention(q, k, v, sel_idx):
    B, S, G, R, dh = q.shape
    n_blk = S // SEL_BLOCK
    n_sel = sel_idx.shape[-1]
    L = n_sel * SEL_BLOCK
    kb = k.reshape(B, n_blk, SEL_BLOCK, G, dh).transpose(0, 3, 1, 2, 4)
    vb = v.reshape(B, n_blk, SEL_BLOCK, G, dh).transpose(0, 3, 1, 2, 4)
    bi = jnp.arange(B)[:, None, None, None]
    gi = jnp.arange(G)[None, :, None, None]
    scale = dh ** -0.5

    def one_chunk(c):
        start = c * SEL_CHUNK
        qc = lax.dynamic_slice_in_dim(q, start, SEL_CHUNK, axis=1)
        ic = lax.dynamic_slice_in_dim(sel_idx, start, SEL_CHUNK, axis=2)
        kg = kb[bi, gi, ic].reshape(B, G, SEL_CHUNK, L, dh)
        vg = vb[bi, gi, ic].reshape(B, G, SEL_CHUNK, L, dh)
        s = jnp.einsum('bqgrd,bgqld->bgrql', qc, kg, preferred_element_type=jnp.float32) * scale
        kpos = (ic[..., None] * SEL_BLOCK + jnp.arange(SEL_BLOCK)).reshape(B, G, SEL_CHUNK, L)
        qpos = start + jnp.arange(SEL_CHUNK)
        mask = (kpos <= qpos[:, None])[:, :, None]
        pr = masked_softmax(s, mask)
        return jnp.einsum('bgrql,bgqld->bqgrd', pr.astype(v.dtype), vg)

    out = lax.map(one_chunk, jnp.arange(S // SEL_CHUNK))
    return out.transpose(1, 0, 2, 3, 4, 5).reshape(B, S, G, R, dh)


def nsa_mixer(qn, kc, vc, ks, vs, kw, vw, gn, cos, sin, pos_emb, w_k1, w_k2, w_v1, w_v2):
    B, S, _ = qn.shape
    G, R, dh = NSA_GROUPS, NSA_HEADS // NSA_GROUPS, HEAD_DIM
    q = qn.reshape(B, S, NSA_HEADS, dh)
    q_nope = q.reshape(B, S, G, R, dh)
    q_rope = apply_rope(q, cos, sin).reshape(B, S, G, R, dh)
    t = jnp.arange(S)
    k_c = compress_tokens(kc.reshape(B, S, G, dh), pos_emb[0], w_k1, w_k2)
    v_c = compress_tokens(vc.reshape(B, S, G, dh), pos_emb[1], w_v1, w_v2)
    n_cmp = k_c.shape[1]
    s_c = jnp.einsum('bqgrd,bcgd->bgrqc', q_nope, k_c, preferred_element_type=jnp.float32) * dh ** -0.5
    cmp_end = jnp.arange(n_cmp) * CMP_STRIDE + CMP_BLOCK - 1
    p_c = masked_softmax(s_c, cmp_end[None, :] <= t[:, None])
    o_cmp = jnp.einsum('bgrqc,bcgd->bqgrd', p_c.astype(v_c.dtype), v_c)
    n_blk = S // SEL_BLOCK
    c_start = jnp.arange(n_cmp) * CMP_STRIDE
    b_start = jnp.arange(n_blk) * SEL_BLOCK
    overlap = ((c_start[:, None] < b_start[None, :] + SEL_BLOCK) & (c_start[:, None] + CMP_BLOCK > b_start[None, :])).astype(jnp.float32)
    imp = jnp.einsum('bgrqc,cj->bgqj', p_c, overlap)
    blk = jnp.arange(n_blk)[None, :]
    cur = (t // SEL_BLOCK)[:, None]
    forced = (blk == 0) | (blk == cur) | (blk == cur - 1)
    imp = jnp.where(forced, FORCE, jnp.where(blk <= cur, imp, NEG_INF))
    n_sel = min(N_SEL, n_blk)
    _, sel_idx = lax.top_k(imp, n_sel)
    k_s = apply_rope(ks.reshape(B, S, G, dh), cos, sin)
    o_slc = selected_attention(q_rope, k_s, vs.reshape(B, S, G, dh), sel_idx)
    k_w = apply_rope(kw.reshape(B, S, G, dh), cos, sin)
    o_win = banded_attention(q_rope, k_w, vw.reshape(B, S, G, dh), NSA_WINDOW)
    g = jax.nn.sigmoid(gn).reshape(B, S, G, R, N_NSA_BRANCH)
    o = g[..., 0:1] * o_cmp + g[..., 1:2] * o_slc + g[..., 2:3] * o_win
    return o.reshape(B, S, NSA_Q)


def temporal_mix(h, cos, sin, w_in, sinks, pos_emb, w_k1, w_k2, w_v1, w_v2, w_proj_swa, w_proj_nsa, w_out):
    B, S, _ = h.shape
    dh = HEAD_DIM
    r_a = SWA_HEADS // SWA_KV_HEADS
    z = h @ w_in
    qa, ka, va, qn, kc, vc, ks, vs, kw, vw, gn, gm = split_in(z)
    qa = apply_rope(qa.reshape(B, S, SWA_HEADS, dh), cos, sin).reshape(B, S, SWA_KV_HEADS, r_a, dh)
    ka = apply_rope(ka.reshape(B, S, SWA_KV_HEADS, dh), cos, sin)
    va = va.reshape(B, S, SWA_KV_HEADS, dh)
    sink = sinks.astype(jnp.float32).reshape(1, SWA_KV_HEADS, r_a, 1, 1)
    o_a = banded_attention(qa, ka, va, SWA_WINDOW, sink).reshape(B, S, SWA_Q)
    o_b = nsa_mixer(qn, kc, vc, ks, vs, kw, vw, gn, cos, sin, pos_emb, w_k1, w_k2, w_v1, w_v2)
    gates = jax.nn.sigmoid(gm).reshape(B, S, N_BRANCH, D_MODEL)
    y = gates[:, :, 0] * (o_a @ w_proj_swa) + gates[:, :, 1] * (o_b @ w_proj_nsa)
    return y @ w_out


def _w(k, shape, fan_in, scale=1.0):
    return jax.random.normal(k, shape, jnp.float32) * (scale * fan_in ** -0.5)


def setup_inputs(seed: int = 0) -> dict:
    key = jax.random.key(seed)
    ks = jax.random.split(key, 24)
    beta = (8.0 * DEPTH) ** -0.25
    col_scale = jnp.concatenate([jnp.full((n,), beta if isv else 1.0, jnp.float32) for n, isv in zip(IN_SPLITS, IN_IS_VALUE)])
    return {
        'x': jax.random.normal(ks[0], (BATCH, SEQ, D_MODEL), jnp.float32),
        'p': jax.random.normal(ks[1], (DEPTH, BATCH, SEQ, PLE_DIM), jnp.float32),
        'positions': jnp.broadcast_to(jnp.arange(SEQ, dtype=jnp.int32), (BATCH, SEQ)),
        'w_in': _w(ks[2], (DEPTH, D_MODEL, D_IN), D_MODEL) * col_scale,
        'attn_sinks': 0.5 * jax.random.normal(ks[3], (DEPTH, SWA_HEADS), jnp.float32),
        'cmp_pos_emb': 0.1 * jax.random.normal(ks[4], (DEPTH, 2, CMP_BLOCK, HEAD_DIM), jnp.float32),
        'w_cmp_k1': _w(ks[5], (DEPTH, CMP_BLOCK * HEAD_DIM, CMP_HIDDEN), CMP_BLOCK * HEAD_DIM),
        'w_cmp_k2': _w(ks[6], (DEPTH, CMP_HIDDEN, HEAD_DIM), CMP_HIDDEN),
        'w_cmp_v1': _w(ks[7], (DEPTH, CMP_BLOCK * HEAD_DIM, CMP_HIDDEN), CMP_BLOCK * HEAD_DIM),
        'w_cmp_v2': _w(ks[8], (DEPTH, CMP_HIDDEN, HEAD_DIM), CMP_HIDDEN, beta),
        'w_proj_swa': _w(ks[9], (DEPTH, SWA_Q, D_MODEL), SWA_Q, beta),
        'w_proj_nsa': _w(ks[10], (DEPTH, NSA_Q, D_MODEL), NSA_Q, beta),
        'w_out': _w(ks[11], (DEPTH, D_MODEL, D_MODEL), D_MODEL, beta),
        'ln1_g': 1.0 + 0.02 * jax.random.normal(ks[12], (DEPTH, D_MODEL), jnp.float32),
        'ln1_b': 0.02 * jax.random.normal(ks[13], (DEPTH, D_MODEL), jnp.float32),
        'w_ff_gate': _w(ks[14], (DEPTH, D_MODEL, D_FF), D_MODEL, beta),
        'w_ff_up': _w(ks[15], (DEPTH, D_MODEL, D_FF), D_MODEL, beta),
        'w_ff_down': _w(ks[16], (DEPTH, D_FF, D_MODEL), D_FF, beta),
        'w_ple': _w(ks[17], (DEPTH, PLE_DIM, D_MODEL), PLE_DIM, beta),
        'w_ple_gate': _w(ks[18], (DEPTH, D_MODEL, D_MODEL), D_MODEL),
        'ln2_g': 1.0 + 0.02 * jax.random.normal(ks[19], (DEPTH, D_MODEL), jnp.float32),
        'ln2_b': 0.02 * jax.random.normal(ks[20], (DEPTH, D_MODEL), jnp.float32),
    }


def reference(x, p, positions, w_in, attn_sinks, cmp_pos_emb, w_cmp_k1, w_cmp_k2, w_cmp_v1, w_cmp_v2, w_proj_swa, w_proj_nsa, w_out, ln1_g, ln1_b, w_ff_gate, w_ff_up, w_ff_down, w_ple, w_ple_gate, ln2_g, ln2_b):
    alpha = (2.0 * DEPTH) ** 0.25
    cos, sin = rope_tables(positions, x.dtype)
    h = x
    for i in range(DEPTH):
        mix = temporal_mix(h, cos, sin, w_in[i], attn_sinks[i], cmp_pos_emb[i], w_cmp_k1[i], w_cmp_k2[i], w_cmp_v1[i], w_cmp_v2[i], w_proj_swa[i], w_proj_nsa[i], w_out[i])
        h = layer_norm(alpha * h + mix, ln1_g[i], ln1_b[i])
        ff = (jax.nn.silu(h @ w_ff_gate[i]) * (h @ w_ff_up[i])) @ w_ff_down[i]
        ple = jax.nn.sigmoid(h @ w_ple_gate[i]) * (p[i] @ w_ple[i])
        h = layer_norm(alpha * h + ff + ple, ln2_g[i], ln2_b[i])
    return h
```

```python
import functools

import jax
import jax.numpy as jnp
from jax import lax
from jax.experimental import pallas as pl
from jax.experimental.pallas import tpu as pltpu

F32 = jnp.float32
BF16 = jnp.bfloat16

D_MODEL = 1024
HEAD_DIM = 64
HALF_DIM = HEAD_DIM // 2
ROPE_THETA = 10000.0
LN_EPS = 1e-5
NEG_INF = -1e30
FORCE = 1e30
TINY = 1e-30
PLE_DIM = 256

SWA_HEADS = 8
SWA_WINDOW = 128
NSA_HEADS = 8
NSA_GROUPS = 2
NSA_REP = NSA_HEADS // NSA_GROUPS
NSA_WINDOW = 512
CMP_BLOCK = 32
CMP_STRIDE = 16
CMP_HIDDEN = 256
SEL_BLOCK = 64
N_SEL = 16
D_FF = 2816

SWA_Q = SWA_HEADS * HEAD_DIM
NSA_Q = NSA_HEADS * HEAD_DIM
NSA_KV = NSA_GROUPS * HEAD_DIM
NSA_GATES = NSA_HEADS * 3
GATE_ROWS = 32
MAX_SEL_BLOCKS = 64

LANES = 128
VMEM_LIMIT = 56 * 1024 * 1024

TOK_K_COLS = 5 * LANES
TOK_COLS = TOK_K_COLS + 2 * D_MODEL
FEAT_V_ROWS = 2 * NSA_KV + HEAD_DIM + GATE_ROWS + 32
FEAT_ROWS = SWA_Q + NSA_Q + FEAT_V_ROWS

INPROJ_TM = 512
ATTN_TQ = 256
SWA_TQ = 128
MLP_TM = 512


def _dot(a, b):
    return jnp.dot(a, b, preferred_element_type=F32)


def _dot_nt(a, b):
    return lax.dot_general(a, b, (((1,), (1,)), ((), ())), preferred_element_type=F32)


def _inproj_kernel(x_ref, posc_ref, posr_ref, invr_ref, invc_ref, wn_ref, wt_ref,
                   ks_ref, kw_ref, ka_ref, kc_ref, vc_ref, gm_ref,
                   qaT_ref, qnrT_ref, qnnT_ref, vsT_ref, vwT_ref, vaT_ref, gnT_ref):
    tm = x_ref.shape[1]
    xb = x_ref[0].astype(BF16)

    ang = posc_ref[0] * invr_ref[...]
    cos_t = jnp.cos(ang)
    sin_t = jnp.sin(ang)
    lane = lax.broadcasted_iota(jnp.int32, (tm, LANES), 1)
    first = (lane & (HEAD_DIM - 1)) < HALF_DIM
    sin_s = jnp.where(first, -sin_t, sin_t)

    def rope_tok(z):
        partner = jnp.where(first, pltpu.roll(z, LANES - HALF_DIM, 1), pltpu.roll(z, HALF_DIM, 1))
        return z * cos_t + partner * sin_s

    zk = _dot(xb, wn_ref[:, 0:TOK_K_COLS])
    ks_ref[0] = rope_tok(zk[:, 0:128]).astype(BF16)
    kw_ref[0] = rope_tok(zk[:, 128:256]).astype(BF16)
    ka_ref[0] = rope_tok(zk[:, 256:384]).astype(BF16)
    kc_ref[0] = zk[:, 384:512]
    vc_ref[0] = zk[:, 512:640]
    gchunk = 512
    for c in range(2 * D_MODEL // gchunk):
        zg = _dot(xb, wn_ref[:, TOK_K_COLS + c * gchunk:TOK_K_COLS + (c + 1) * gchunk])
        gm_ref[0, :, c * gchunk:(c + 1) * gchunk] = jax.nn.sigmoid(zg).astype(BF16)

    ang_t = invc_ref[...] * posr_ref[0]
    cos_f = jnp.cos(ang_t)
    sin_f = jnp.sin(ang_t)
    scale = HEAD_DIM ** -0.5

    def rope_feat_store(z, out_ref, n_heads):
        for h in range(n_heads):
            x1 = z[HEAD_DIM * h:HEAD_DIM * h + HALF_DIM]
            x2 = z[HEAD_DIM * h + HALF_DIM:HEAD_DIM * (h + 1)]
            out_ref[0, HEAD_DIM * h:HEAD_DIM * h + HALF_DIM, :] = ((x1 * cos_f - x2 * sin_f) * scale).astype(BF16)
            out_ref[0, HEAD_DIM * h + HALF_DIM:HEAD_DIM * (h + 1), :] = ((x2 * cos_f + x1 * sin_f) * scale).astype(BF16)

    zqa = _dot_nt(wt_ref[0:SWA_Q, :], xb)
    rope_feat_store(zqa, qaT_ref, SWA_HEADS)
    zqn = _dot_nt(wt_ref[SWA_Q:SWA_Q + NSA_Q, :], xb)
    rope_feat_store(zqn, qnrT_ref, NSA_HEADS)
    qnnT_ref[0] = (zqn * scale).astype(BF16)
    zv = _dot_nt(wt_ref[SWA_Q + NSA_Q:FEAT_ROWS, :], xb)
    for j in range(tm // ATTN_TQ):
        sl = slice(j * ATTN_TQ, (j + 1) * ATTN_TQ)
        vsT_ref[0, j] = zv[0:NSA_KV, sl].astype(BF16)
        vwT_ref[0, j] = zv[NSA_KV:2 * NSA_KV, sl].astype(BF16)
    for j in range(tm // SWA_TQ):
        sl = slice(j * SWA_TQ, (j + 1) * SWA_TQ)
        vaT_ref[0, j] = zv[2 * NSA_KV:2 * NSA_KV + HEAD_DIM, sl].astype(BF16)
    g0 = 2 * NSA_KV + HEAD_DIM
    gnT_ref[0] = jax.nn.sigmoid(zv[g0:g0 + GATE_ROWS])


def _inproj(x, positions, w_in):
    B, S, D = x.shape
    tm = INPROJ_TM
    offs = {}
    off = 0
    for name, n in (("qa", SWA_Q), ("ka", HEAD_DIM), ("va", HEAD_DIM), ("qn", NSA_Q), ("kc", NSA_KV),
                    ("vc", NSA_KV), ("ks", NSA_KV), ("vs", NSA_KV), ("kw", NSA_KV), ("vw", NSA_KV),
                    ("gn", NSA_GATES), ("gm", 2 * D_MODEL)):
        offs[name] = (off, off + n)
        off += n
    col = lambda name: w_in[:, offs[name][0]:offs[name][1]]
    zpad = lambda n: jnp.zeros((D, n), w_in.dtype)
    wn = jnp.concatenate([col("ks"), col("kw"), col("ka"), zpad(LANES - HEAD_DIM), col("kc"), col("vc"),
                          col("gm")], axis=1).astype(BF16)
    wt = jnp.concatenate([col("qa"), col("qn"), col("vs"), col("vw"), col("va"), col("gn"),
                          zpad(FEAT_ROWS - (SWA_Q + NSA_Q + 2 * NSA_KV + HEAD_DIM + NSA_GATES))],
                         axis=1).T.astype(BF16)
    assert wn.shape == (D, TOK_COLS) and wt.shape == (FEAT_ROWS, D)

    inv = ROPE_THETA ** (-jnp.arange(0, HEAD_DIM, 2, dtype=F32) / HEAD_DIM)
    inv_row = jnp.tile(inv, LANES // HALF_DIM)[None, :]
    inv_col = inv[:, None]
    posf = positions.astype(F32)
    pos_col = posf[:, :, None]
    pos_row = posf[:, None, :]

    tok = lambda w: pl.BlockSpec((1, tm, w), lambda b, i: (b, i, 0))
    feat = lambda r: pl.BlockSpec((1, r, tm), lambda b, i: (b, 0, i))
    const = lambda shp: pl.BlockSpec(shp, lambda b, i: (0,) * len(shp))
    out_shape = (
        jax.ShapeDtypeStruct((B, S, LANES), BF16),
        jax.ShapeDtypeStruct((B, S, LANES), BF16),
        jax.ShapeDtypeStruct((B, S, LANES), BF16),
        jax.ShapeDtypeStruct((B, S, LANES), F32),
        jax.ShapeDtypeStruct((B, S, LANES), F32),
        jax.ShapeDtypeStruct((B, S, 2 * D_MODEL), BF16),
        jax.ShapeDtypeStruct((B, SWA_Q, S), BF16),
        jax.ShapeDtypeStruct((B, NSA_Q, S), BF16),
        jax.ShapeDtypeStruct((B, NSA_Q, S), BF16),
        jax.ShapeDtypeStruct((B, S // ATTN_TQ, NSA_KV, ATTN_TQ), BF16),
        jax.ShapeDtypeStruct((B, S // ATTN_TQ, NSA_KV, ATTN_TQ), BF16),
        jax.ShapeDtypeStruct((B, S // SWA_TQ, HEAD_DIM, SWA_TQ), BF16),
        jax.ShapeDtypeStruct((B, GATE_ROWS, S), F32),
    )
    out_specs = (
        tok(LANES), tok(LANES), tok(LANES), tok(LANES), tok(LANES), tok(2 * D_MODEL),
        feat(SWA_Q), feat(NSA_Q), feat(NSA_Q),
        pl.BlockSpec((1, tm // ATTN_TQ, NSA_KV, ATTN_TQ), lambda b, i: (b, i, 0, 0)),
        pl.BlockSpec((1, tm // ATTN_TQ, NSA_KV, ATTN_TQ), lambda b, i: (b, i, 0, 0)),
        pl.BlockSpec((1, tm // SWA_TQ, HEAD_DIM, SWA_TQ), lambda b, i: (b, i, 0, 0)),
        feat(GATE_ROWS),
    )
    return pl.pallas_call(
        _inproj_kernel,
        grid=(B, S // tm),
        in_specs=[tok(D), tok(1), pl.BlockSpec((1, 1, tm), lambda b, i: (b, 0, i)),
                  const((1, LANES)), const((HALF_DIM, 1)), const((D, TOK_COLS)), const((FEAT_ROWS, D))],
        out_specs=out_specs,
        out_shape=out_shape,
        compiler_params=pltpu.CompilerParams(dimension_semantics=("parallel", "parallel"),
                                             vmem_limit_bytes=VMEM_LIMIT),
        name="inproj",
    )(x, pos_col, pos_row, inv_row, inv_col, wn, wt)


def _gelu_tanh(x):
    return 0.5 * x * (1.0 + jnp.tanh(0.7978845608028654 * (x + 0.044715 * (x * x * x))))


def _compress_kernel(ck_ref, cv_ref, pe_ref, w1k_ref, w2k_ref, w1v_ref, w2vT_ref, kc_ref, vcT_ref):
    n_ch = ck_ref.shape[2]
    half = CMP_STRIDE * HEAD_DIM

    def hidden(c, pe_a, pe_b, w1_ref):
        a = (c + pe_a).astype(BF16)
        b = (c + pe_b).astype(BF16)
        u = _dot(a, w1_ref[0:half, :])
        v = _dot(b, w1_ref[half:2 * half, :])
        return _gelu_tanh(u + pltpu.roll(v, n_ch - 1, 0)).astype(BF16)

    for g in range(NSA_GROUPS):
        hk = hidden(ck_ref[0, g], pe_ref[0:1, :], pe_ref[1:2, :], w1k_ref)
        kc_ref[0, g] = _dot(hk, w2k_ref[...]).astype(BF16)
        hv = hidden(cv_ref[0, g], pe_ref[2:3, :], pe_ref[3:4, :], w1v_ref)
        vcT_ref[0, g] = _dot_nt(w2vT_ref[...], hv).astype(BF16)


def _compress(kc, vc, pos_emb, w_k1, w_k2, w_v1, w_v2):
    B, S, _ = kc.shape
    n_ch = S // CMP_STRIDE
    half = CMP_STRIDE * HEAD_DIM

    def chunks(a):
        a = a.reshape(B, n_ch, CMP_STRIDE, NSA_GROUPS, HEAD_DIM)
        return a.transpose(0, 3, 1, 2, 4).reshape(B, NSA_GROUPS, n_ch, half)

    pe = pos_emb.reshape(2, 2, half).reshape(4, half)
    const = lambda shp: pl.BlockSpec(shp, lambda b: (0,) * len(shp))
    return pl.pallas_call(
        _compress_kernel,
        grid=(B,),
        in_specs=[pl.BlockSpec((1, NSA_GROUPS, n_ch, half), lambda b: (b, 0, 0, 0)),
                  pl.BlockSpec((1, NSA_GROUPS, n_ch, half), lambda b: (b, 0, 0, 0)),
                  const((4, half)), const((2 * half, CMP_HIDDEN)), const((CMP_HIDDEN, HEAD_DIM)),
                  const((2 * half, CMP_HIDDEN)), const((HEAD_DIM, CMP_HIDDEN))],
        out_specs=(pl.BlockSpec((1, NSA_GROUPS, n_ch, HEAD_DIM), lambda b: (b, 0, 0, 0)),
                   pl.BlockSpec((1, NSA_GROUPS, HEAD_DIM, n_ch), lambda b: (b, 0, 0, 0))),
        out_shape=(jax.ShapeDtypeStruct((B, NSA_GROUPS, n_ch, HEAD_DIM), BF16),
                   jax.ShapeDtypeStruct((B, NSA_GROUPS, HEAD_DIM, n_ch), BF16)),
        compiler_params=pltpu.CompilerParams(dimension_semantics=("parallel",),
                                             vmem_limit_bytes=VMEM_LIMIT),
        name="compress",
    )(chunks(kc), chunks(vc), pe, w_k1.astype(BF16), w_k2.astype(BF16), w_v1.astype(BF16),
      w_v2.T.astype(BF16))


def _store_head_pairs(o_ref, heads):
    for pr in range(len(heads) // 2):
        pair = jnp.concatenate([heads[2 * pr], heads[2 * pr + 1]], axis=0)
        o_ref[0, :, LANES * pr:LANES * (pr + 1)] = pair.T.astype(BF16)


def _cmp_select_kernel(qT_ref, kc_ref, vcT_ref, gT_ref, o_ref, selb_ref, *, n_blk, n_sel):
    tq = qT_ref.shape[2]
    n_ch = kc_ref.shape[2]
    t = pl.program_id(1) * tq + lax.broadcasted_iota(jnp.int32, (1, tq), 1)
    cidx = lax.broadcasted_iota(jnp.int32, (n_ch, 1), 0)
    cmask = (cidx * CMP_STRIDE + (CMP_BLOCK - 1)) <= t
    jj = lax.broadcasted_iota(jnp.int32, (MAX_SEL_BLOCKS, n_ch), 0)
    cc = lax.broadcasted_iota(jnp.int32, (MAX_SEL_BLOCKS, n_ch), 1)
    ratio = SEL_BLOCK // CMP_STRIDE
    ov_t = jnp.where((cc >= ratio * jj - (CMP_BLOCK // CMP_STRIDE - 1)) & (cc <= ratio * jj + ratio - 1),
                     1.0, 0.0).astype(BF16)
    blk = lax.broadcasted_iota(jnp.int32, (MAX_SEL_BLOCKS, tq), 0)
    cur = t >> 6
    forced = (blk == 0) | (blk == cur) | (blk == cur - 1)

    heads = []
    for g in range(NSA_GROUPS):
        psum = jnp.zeros((n_ch, tq), F32)
        for r in range(NSA_REP):
            h = NSA_REP * g + r
            q_t = qT_ref[0, HEAD_DIM * h:HEAD_DIM * (h + 1), :]
            s = jnp.where(cmask, _dot(kc_ref[0, g], q_t), NEG_INF)
            m = jnp.max(s, axis=0, keepdims=True)
            e = jnp.where(cmask, jnp.exp(s - m), 0.0)
            den = jnp.sum(e, axis=0, keepdims=True)
            p = e / jnp.maximum(den, TINY)
            o = _dot(vcT_ref[0, g], p.astype(BF16))
            heads.append(o * gT_ref[0, 3 * h:3 * h + 1, :])
            psum = psum + p
        hi = psum.astype(BF16)
        r1 = psum - hi.astype(F32)
        mid = r1.astype(BF16)
        lo = (r1 - mid.astype(F32)).astype(BF16)
        imp = _dot(ov_t, hi) + _dot(ov_t, mid) + _dot(ov_t, lo)
        imp = jnp.where(forced, FORCE, jnp.where(blk <= cur, imp, NEG_INF))
        rank = jnp.zeros((MAX_SEL_BLOCKS, tq), jnp.int32)
        for i in range(n_blk):
            row = imp[i:i + 1, :]
            rank = rank + jnp.where(blk > i, jnp.where(row >= imp, 1, 0), jnp.where(row > imp, 1, 0))
        selb_ref[0, g] = jnp.where(rank < n_sel, 0.0, NEG_INF).astype(BF16)
    _store_head_pairs(o_ref, heads)


def _cmp_select(qnnT, k_c, v_cT, gnT):
    B, _, S = qnnT.shape
    tq = ATTN_TQ
    n_ch = k_c.shape[2]
    n_blk = S // SEL_BLOCK
    assert n_blk <= MAX_SEL_BLOCKS
    kern = functools.partial(_cmp_select_kernel, n_blk=n_blk, n_sel=min(N_SEL, n_blk))
    return pl.pallas_call(
        kern,
        grid=(B, S // tq),
        in_specs=[pl.BlockSpec((1, NSA_Q, tq), lambda b, i: (b, 0, i)),
                  pl.BlockSpec((1, NSA_GROUPS, n_ch, HEAD_DIM), lambda b, i: (b, 0, 0, 0)),
                  pl.BlockSpec((1, NSA_GROUPS, HEAD_DIM, n_ch), lambda b, i: (b, 0, 0, 0)),
                  pl.BlockSpec((1, GATE_ROWS, tq), lambda b, i: (b, 0, i))],
        out_specs=(pl.BlockSpec((1, tq, NSA_Q), lambda b, i: (b, i, 0)),
                   pl.BlockSpec((1, NSA_GROUPS, MAX_SEL_BLOCKS, tq), lambda b, i: (b, 0, 0, i))),
        out_shape=(jax.ShapeDtypeStruct((B, S, NSA_Q), BF16),
                   jax.ShapeDtypeStruct((B, NSA_GROUPS, MAX_SEL_BLOCKS, S), BF16)),
        compiler_params=pltpu.CompilerParams(dimension_semantics=("parallel", "parallel"),
                                             vmem_limit_bytes=VMEM_LIMIT),
        name="cmp_select",
    )(qnnT, k_c, v_cT, gnT)


def _chunk_update(carry, k_lhs, v_t, rhs, mask):
    m, l, acc = carry
    s = _dot(k_lhs, rhs)
    if mask is not None:
        s = jnp.where(mask, s, NEG_INF)
    m_new = jnp.maximum(m, jnp.max(s, axis=0, keepdims=True))
    p = jnp.exp(s - m_new)
    if mask is not None:
        p = jnp.where(mask, p, 0.0)
    alpha = jnp.exp(m - m_new)
    l_new = alpha * l + jnp.sum(p, axis=0, keepdims=True)
    acc_new = alpha * acc + _dot(v_t, p.astype(BF16))
    return m_new, l_new, acc_new


def _attn_kernel(*refs, n_heads, rep, nprev, selected, use_sink, gate_branch):
    refs = list(refs)
    qT_ref, k_ref, vT_ref = refs[:3]
    pos = 3
    gT_ref = selb_ref = e_ref = sink_ref = None
    if gate_branch is not None:
        gT_ref = refs[pos]; pos += 1
    if selected:
        selb_ref, e_ref = refs[pos], refs[pos + 1]; pos += 2
    if use_sink:
        sink_ref = refs[pos]; pos += 1
    o_ref = refs[pos]

    tq = qT_ref.shape[2]
    i = pl.program_id(1)
    ksub = lax.broadcasted_iota(jnp.int32, (tq, tq), 0)
    qlane = lax.broadcasted_iota(jnp.int32, (tq, tq), 1)
    diag_mask = ksub <= qlane
    part_mask = ksub > qlane
    zeros_h = jnp.zeros((HEAD_DIM, tq), BF16)

    heads = []
    for h in range(n_heads):
        g = h // rep
        q_t = qT_ref[0, HEAD_DIM * h:HEAD_DIM * (h + 1), :]
        parts = [zeros_h, zeros_h]
        parts[g] = q_t
        if selected:
            parts += [selb_ref[0, g], zeros_h]
        rhs = jnp.concatenate(parts, axis=0)

        def load(c, g=g):
            start = pl.multiple_of(c * tq, tq)
            k = k_ref[0, pl.ds(start, tq), :]
            if selected:
                k = jnp.concatenate([k, e_ref[pl.ds(start, tq), :]], axis=1)
            v_t = vT_ref[0, c, HEAD_DIM * g:HEAD_DIM * (g + 1), :]
            return k, v_t

        if use_sink:
            carry = (jnp.full((1, tq), sink_ref[h], F32), jnp.ones((1, tq), F32),
                     jnp.zeros((HEAD_DIM, tq), F32))
        else:
            carry = (jnp.full((1, tq), NEG_INF, F32), jnp.zeros((1, tq), F32),
                     jnp.zeros((HEAD_DIM, tq), F32))
        if selected:
            def body(c, cr, rhs=rhs, load=load):
                k, v_t = load(c)
                return _chunk_update(cr, k, v_t, rhs, None)
            carry = lax.fori_loop(0, i, body, carry)
        else:
            for d in range(nprev, 0, -1):
                valid = i >= d
                mask = jnp.logical_and(part_mask, valid) if d == nprev else jnp.broadcast_to(valid, (tq, tq))
                k, v_t = load(jnp.maximum(i - d, 0))
                carry = _chunk_update(carry, k, v_t, rhs, mask)
        k, v_t = load(i)
        m, l, acc = _chunk_update(carry, k, v_t, rhs, diag_mask)
        o = acc * (1.0 / jnp.maximum(l, TINY))
        if gate_branch is not None:
            o = o * gT_ref[0, 3 * h + gate_branch:3 * h + gate_branch + 1, :]
        heads.append(o)
    _store_head_pairs(o_ref, heads)


def _attention(qT, k, vT, *, tq, rep, nprev=0, selected=False, gnT=None, gate_branch=None,
               selb=None, sinks=None):
    B, F, S = qT.shape
    n_heads = F // HEAD_DIM
    n_kv = vT.shape[2]
    assert vT.shape == (B, S // tq, n_kv, tq)
    in_specs = [pl.BlockSpec((1, F, tq), lambda b, i: (b, 0, i)),
                pl.BlockSpec((1, S, LANES), lambda b, i: (b, 0, 0)),
                pl.BlockSpec((1, S // tq, n_kv, tq), lambda b, i: (b, 0, 0, 0))]
    args = [qT, k, vT]
    if gate_branch is not None:
        in_specs.append(pl.BlockSpec((1, GATE_ROWS, tq), lambda b, i: (b, 0, i)))
        args.append(gnT)
    if selected:
        onehot = (jnp.arange(S)[:, None] // SEL_BLOCK == jnp.arange(LANES)[None, :]).astype(BF16)
        in_specs += [pl.BlockSpec((1, NSA_GROUPS, MAX_SEL_BLOCKS, tq), lambda b, i: (b, 0, 0, i)),
                     pl.BlockSpec((S, LANES), lambda b, i: (0, 0))]
        args += [selb, onehot]
    if sinks is not None:
        in_specs.append(pl.BlockSpec(memory_space=pltpu.SMEM))
        args.append(sinks)
    kern = functools.partial(_attn_kernel, n_heads=n_heads, rep=rep, nprev=nprev, selected=selected,
                             use_sink=sinks is not None, gate_branch=gate_branch)
    return pl.pallas_call(
        kern,
        grid=(B, S // tq),
        in_specs=in_specs,
        out_specs=pl.BlockSpec((1, tq, F), lambda b, i: (b, i, 0)),
        out_shape=jax.ShapeDtypeStruct((B, S, F), BF16),
        compiler_params=pltpu.CompilerParams(dimension_semantics=("parallel", "parallel"),
                                             vmem_limit_bytes=VMEM_LIMIT),
        name="attn_sel" if selected else ("attn_swa" if sinks is not None else "attn_win"),
    )(*args)


def _layer_norm(r, g, b):
    mu = jnp.mean(r, axis=-1, keepdims=True)
    d = r - mu
    var = jnp.mean(d * d, axis=-1, keepdims=True)
    return d * lax.rsqrt(var + LN_EPS) * g + b


def _outproj_kernel(x_ref, oa_ref, oc_ref, os_ref, ow_ref, gm_ref, wa_ref, wb_ref, wo_ref, g_ref, b_ref,
                    h_ref, *, alpha):
    o_b = (oc_ref[...].astype(F32) + os_ref[...].astype(F32) + ow_ref[...].astype(F32)).astype(BF16)
    y = (gm_ref[:, 0:D_MODEL].astype(F32) * _dot(oa_ref[...], wa_ref[...])
         + gm_ref[:, D_MODEL:2 * D_MODEL].astype(F32) * _dot(o_b, wb_ref[...]))
    mix = _dot(y.astype(BF16), wo_ref[...])
    h_ref[...] = _layer_norm(alpha * x_ref[...] + mix, g_ref[...], b_ref[...])


def _outproj(x2, o_a, o_c, o_s, o_w, gm, w_a, w_b, w_o, ln_g, ln_b, alpha):
    T, D = x2.shape
    tm = MLP_TM
    row = lambda w: pl.BlockSpec((tm, w), lambda i: (i, 0))
    const = lambda shp: pl.BlockSpec(shp, lambda i: (0, 0))
    return pl.pallas_call(
        functools.partial(_outproj_kernel, alpha=alpha),
        grid=(T // tm,),
        in_specs=[row(D), row(SWA_Q), row(NSA_Q), row(NSA_Q), row(NSA_Q), row(2 * D),
                  const((SWA_Q, D)), const((NSA_Q, D)), const((D, D)), const((1, D)), const((1, D))],
        out_specs=row(D),
        out_shape=jax.ShapeDtypeStruct((T, D), F32),
        compiler_params=pltpu.CompilerParams(dimension_semantics=("parallel",),
                                             vmem_limit_bytes=VMEM_LIMIT),
        name="outproj_ln1",
    )(x2, o_a, o_c, o_s, o_w, gm, w_a.astype(BF16), w_b.astype(BF16), w_o.astype(BF16),
      ln_g[None, :], ln_b[None, :])


def _ffn_kernel(h_ref, p_ref, wg_ref, wu_ref, wd_ref, wpg_ref, wple_ref, g_ref, b_ref, o_ref, *,
                alpha, ff_chunk):
    h = h_ref[...]
    hb = h.astype(BF16)
    ff = jnp.zeros(h.shape, F32)
    for c in range(D_FF // ff_chunk):
        sl = slice(c * ff_chunk, (c + 1) * ff_chunk)
        a = jax.nn.silu(_dot(hb, wg_ref[:, sl])) * _dot(hb, wu_ref[:, sl])
        ff = ff + _dot(a.astype(BF16), wd_ref[sl, :])
    ple = jax.nn.sigmoid(_dot(hb, wpg_ref[...])) * _dot(p_ref[...].astype(BF16), wple_ref[...])
    o_ref[...] = _layer_norm(alpha * h + ff + ple, g_ref[...], b_ref[...])


def _ffn(h1, p2, w_g, w_u, w_d, w_pg, w_ple, ln_g, ln_b, alpha):
    T, D = h1.shape
    tm = MLP_TM
    row = lambda w: pl.BlockSpec((tm, w), lambda i: (i, 0))
    const = lambda shp: pl.BlockSpec(shp, lambda i: (0, 0), pipeline_mode=pl.Buffered(1))
    return pl.pallas_call(
        functools.partial(_ffn_kernel, alpha=alpha, ff_chunk=D_FF // 2),
        grid=(T // tm,),
        in_specs=[row(D), row(PLE_DIM), const((D, D_FF)), const((D, D_FF)), const((D_FF, D)),
                  const((D, D)), const((PLE_DIM, D)), const((1, D)), const((1, D))],
        out_specs=row(D),
        out_shape=jax.ShapeDtypeStruct((T, D), F32),
        compiler_params=pltpu.CompilerParams(dimension_semantics=("parallel",),
                                             vmem_limit_bytes=VMEM_LIMIT),
        name="ffn_ple_ln2",
    )(h1, p2, w_g.astype(BF16), w_u.astype(BF16), w_d.astype(BF16), w_pg.astype(BF16),
      w_ple.astype(BF16), ln_g[None, :], ln_b[None, :])


def kernel(x, p, positions, w_in, attn_sinks, cmp_pos_emb, w_cmp_k1, w_cmp_k2, w_cmp_v1, w_cmp_v2, w_proj_swa, w_proj_nsa, w_out, ln1_g, ln1_b, w_ff_gate, w_ff_up, w_ff_down, w_ple, w_ple_gate, ln2_g, ln2_b):
    B, S, D = x.shape
    depth = w_in.shape[0]
    assert D == D_MODEL and S % INPROJ_TM == 0 and S >= NSA_WINDOW + ATTN_TQ
    alpha = (2.0 * depth) ** 0.25
    h = x
    for i in range(depth):
        (ks, kw, ka, kc, vc, gm, qaT, qnrT, qnnT, vsT, vwT, vaT, gnT) = _inproj(h, positions, w_in[i])
        k_c, v_cT = _compress(kc, vc, cmp_pos_emb[i], w_cmp_k1[i], w_cmp_k2[i], w_cmp_v1[i], w_cmp_v2[i])
        o_cmp, selb = _cmp_select(qnnT, k_c, v_cT, gnT)
        o_slc = _attention(qnrT, ks, vsT, tq=ATTN_TQ, rep=NSA_REP, selected=True, gnT=gnT, gate_branch=1,
                           selb=selb)
        o_win = _attention(qnrT, kw, vwT, tq=ATTN_TQ, rep=NSA_REP, nprev=NSA_WINDOW // ATTN_TQ, gnT=gnT,
                           gate_branch=2)
        o_swa = _attention(qaT, ka, vaT, tq=SWA_TQ, rep=SWA_HEADS, nprev=SWA_WINDOW // SWA_TQ,
                           sinks=attn_sinks[i].astype(F32))
        flat = lambda a: a.reshape(B * S, a.shape[-1])
        h1 = _outproj(flat(h), flat(o_swa), flat(o_cmp), flat(o_slc), flat(o_win), flat(gm),
                      w_proj_swa[i], w_proj_nsa[i], w_out[i], ln1_g[i], ln1_b[i], alpha)
        h2 = _ffn(h1, flat(p[i]), w_ff_gate[i], w_ff_up[i], w_ff_down[i], w_ple_gate[i], w_ple[i],
                  ln2_g[i], ln2_b[i], alpha)
        h = h2.reshape(B, S, D)
    return h
```

```python
import functools

import jax
import jax.numpy as jnp
from jax import lax
from jax.experimental import pallas as pl
from jax.experimental.pallas import tpu as pltpu

F32 = jnp.float32
BF16 = jnp.bfloat16

D_MODEL = 1024
HEAD_DIM = 64
HALF_DIM = HEAD_DIM // 2
ROPE_THETA = 10000.0
LN_EPS = 1e-5
NEG_INF = -1e30
FORCE = 1e30
TINY = 1e-30
PLE_DIM = 256

SWA_HEADS = 8
SWA_WINDOW = 128
NSA_HEADS = 8
NSA_GROUPS = 2
NSA_REP = NSA_HEADS // NSA_GROUPS
NSA_WINDOW = 512
CMP_BLOCK = 32
CMP_STRIDE = 16
CMP_HIDDEN = 256
SEL_BLOCK = 64
N_SEL = 16
D_FF = 2816

SWA_Q = SWA_HEADS * HEAD_DIM
NSA_Q = NSA_HEADS * HEAD_DIM
NSA_KV = NSA_GROUPS * HEAD_DIM
NSA_GATES = NSA_HEADS * 3
GATE_ROWS = 32
MAX_SEL_BLOCKS = 64
DEN_ROWS = 16
LOG2E = 1.4426950408889634

LANES = 128
VMEM_LIMIT = 56 * 1024 * 1024

TOK_K_COLS = 5 * LANES
TOK_COLS = TOK_K_COLS + 2 * D_MODEL
FEAT_V_ROWS = 2 * NSA_KV + HEAD_DIM + GATE_ROWS + 32
FEAT_ROWS = SWA_Q + NSA_Q + FEAT_V_ROWS

INPROJ_TM = 512
ATTN_TQ = 256
SWA_TQ = 128
MLP_TM = 512
ATTN_HEAD_BLOCK = 8


def _dot(a, b):
    return jnp.dot(a, b, preferred_element_type=F32)


def _dot_nt(a, b):
    return lax.dot_general(a, b, (((1,), (1,)), ((), ())), preferred_element_type=F32)


def _inproj_kernel(x_ref, posc_ref, posr_ref, invr_ref, invc_ref, wn_ref, wt_ref,
                   ks_ref, kw_ref, ka_ref, kc_ref, vc_ref, gm_ref,
                   qaT_ref, qnrT_ref, qnnT_ref, vsT_ref, vwT_ref, vaT_ref, gnT_ref):
    tm = x_ref.shape[1]
    xb = x_ref[0].astype(BF16)

    ang = posc_ref[0] * invr_ref[...]
    cos_t = jnp.cos(ang)
    sin_t = jnp.sin(ang)
    lane = lax.broadcasted_iota(jnp.int32, (tm, LANES), 1)
    first = (lane & (HEAD_DIM - 1)) < HALF_DIM
    sin_s = jnp.where(first, -sin_t, sin_t)

    def rope_tok(z):
        partner = jnp.where(first, pltpu.roll(z, LANES - HALF_DIM, 1), pltpu.roll(z, HALF_DIM, 1))
        return z * cos_t + partner * sin_s

    zk = _dot(xb, wn_ref[:, 0:TOK_K_COLS])
    ks_ref[0] = rope_tok(zk[:, 0:128]).astype(BF16)
    kw_ref[0] = rope_tok(zk[:, 128:256]).astype(BF16)
    ka_ref[0] = rope_tok(zk[:, 256:384]).astype(BF16)
    kc_ref[0] = zk[:, 384:512]
    vc_ref[0] = zk[:, 512:640]
    gchunk = 512
    for c in range(2 * D_MODEL // gchunk):
        zg = _dot(xb, wn_ref[:, TOK_K_COLS + c * gchunk:TOK_K_COLS + (c + 1) * gchunk])
        gm_ref[0, :, c * gchunk:(c + 1) * gchunk] = jax.nn.sigmoid(zg).astype(BF16)

    ang_t = invc_ref[...] * posr_ref[0]
    cos_f = jnp.cos(ang_t)
    sin_f = jnp.sin(ang_t)
    scale = HEAD_DIM ** -0.5 * LOG2E

    def rope_feat_store(z, out_ref, n_heads):
        for h in range(n_heads):
            x1 = z[HEAD_DIM * h:HEAD_DIM * h + HALF_DIM]
            x2 = z[HEAD_DIM * h + HALF_DIM:HEAD_DIM * (h + 1)]
            out_ref[0, HEAD_DIM * h:HEAD_DIM * h + HALF_DIM, :] = ((x1 * cos_f - x2 * sin_f) * scale).astype(BF16)
            out_ref[0, HEAD_DIM * h + HALF_DIM:HEAD_DIM * (h + 1), :] = ((x2 * cos_f + x1 * sin_f) * scale).astype(BF16)

    zqa = _dot_nt(wt_ref[0:SWA_Q, :], xb)
    rope_feat_store(zqa, qaT_ref, SWA_HEADS)
    zqn = _dot_nt(wt_ref[SWA_Q:SWA_Q + NSA_Q, :], xb)
    rope_feat_store(zqn, qnrT_ref, NSA_HEADS)
    qnnT_ref[0] = (zqn * scale).astype(BF16)
    zv = _dot_nt(wt_ref[SWA_Q + NSA_Q:FEAT_ROWS, :], xb)
    for j in range(tm // ATTN_TQ):
        sl = slice(j * ATTN_TQ, (j + 1) * ATTN_TQ)
        vsT_ref[0, j] = zv[0:NSA_KV, sl].astype(BF16)
        vwT_ref[0, j] = zv[NSA_KV:2 * NSA_KV, sl].astype(BF16)
    for j in range(tm // SWA_TQ):
        sl = slice(j * SWA_TQ, (j + 1) * SWA_TQ)
        vaT_ref[0, j] = zv[2 * NSA_KV:2 * NSA_KV + HEAD_DIM, sl].astype(BF16)
    g0 = 2 * NSA_KV + HEAD_DIM
    gnT_ref[0] = jax.nn.sigmoid(zv[g0:g0 + GATE_ROWS])


def _inproj(x, positions, w_in):
    B, S, D = x.shape
    tm = INPROJ_TM
    offs = {}
    off = 0
    for name, n in (("qa", SWA_Q), ("ka", HEAD_DIM), ("va", HEAD_DIM), ("qn", NSA_Q), ("kc", NSA_KV),
                    ("vc", NSA_KV), ("ks", NSA_KV), ("vs", NSA_KV), ("kw", NSA_KV), ("vw", NSA_KV),
                    ("gn", NSA_GATES), ("gm", 2 * D_MODEL)):
        offs[name] = (off, off + n)
        off += n
    col = lambda name: w_in[:, offs[name][0]:offs[name][1]]
    zpad = lambda n: jnp.zeros((D, n), w_in.dtype)
    wn = jnp.concatenate([col("ks"), col("kw"), col("ka"), zpad(LANES - HEAD_DIM), col("kc"), col("vc"),
                          col("gm")], axis=1).astype(BF16)
    wt = jnp.concatenate([col("qa"), col("qn"), col("vs"), col("vw"), col("va"), col("gn"),
                          zpad(FEAT_ROWS - (SWA_Q + NSA_Q + 2 * NSA_KV + HEAD_DIM + NSA_GATES))],
                         axis=1).T.astype(BF16)
    assert wn.shape == (D, TOK_COLS) and wt.shape == (FEAT_ROWS, D)

    inv = ROPE_THETA ** (-jnp.arange(0, HEAD_DIM, 2, dtype=F32) / HEAD_DIM)
    inv_row = jnp.tile(inv, LANES // HALF_DIM)[None, :]
    inv_col = inv[:, None]
    posf = positions.astype(F32)
    pos_col = posf[:, :, None]
    pos_row = posf[:, None, :]

    tok = lambda w: pl.BlockSpec((1, tm, w), lambda b, i: (b, i, 0))
    feat = lambda r: pl.BlockSpec((1, r, tm), lambda b, i: (b, 0, i))
    const = lambda shp: pl.BlockSpec(shp, lambda b, i: (0,) * len(shp))
    out_shape = (
        jax.ShapeDtypeStruct((B, S, LANES), BF16),
        jax.ShapeDtypeStruct((B, S, LANES), BF16),
        jax.ShapeDtypeStruct((B, S, LANES), BF16),
        jax.ShapeDtypeStruct((B, S, LANES), F32),
        jax.ShapeDtypeStruct((B, S, LANES), F32),
        jax.ShapeDtypeStruct((B, S, 2 * D_MODEL), BF16),
        jax.ShapeDtypeStruct((B, SWA_Q, S), BF16),
        jax.ShapeDtypeStruct((B, NSA_Q, S), BF16),
        jax.ShapeDtypeStruct((B, NSA_Q, S), BF16),
        jax.ShapeDtypeStruct((B, S // ATTN_TQ, NSA_KV, ATTN_TQ), BF16),
        jax.ShapeDtypeStruct((B, S // ATTN_TQ, NSA_KV, ATTN_TQ), BF16),
        jax.ShapeDtypeStruct((B, S // SWA_TQ, HEAD_DIM, SWA_TQ), BF16),
        jax.ShapeDtypeStruct((B, GATE_ROWS, S), F32),
    )
    out_specs = (
        tok(LANES), tok(LANES), tok(LANES), tok(LANES), tok(LANES), tok(2 * D_MODEL),
        feat(SWA_Q), feat(NSA_Q), feat(NSA_Q),
        pl.BlockSpec((1, tm // ATTN_TQ, NSA_KV, ATTN_TQ), lambda b, i: (b, i, 0, 0)),
        pl.BlockSpec((1, tm // ATTN_TQ, NSA_KV, ATTN_TQ), lambda b, i: (b, i, 0, 0)),
        pl.BlockSpec((1, tm // SWA_TQ, HEAD_DIM, SWA_TQ), lambda b, i: (b, i, 0, 0)),
        feat(GATE_ROWS),
    )
    return pl.pallas_call(
        _inproj_kernel,
        grid=(B, S // tm),
        in_specs=[tok(D), tok(1), pl.BlockSpec((1, 1, tm), lambda b, i: (b, 0, i)),
                  const((1, LANES)), const((HALF_DIM, 1)), const((D, TOK_COLS)), const((FEAT_ROWS, D))],
        out_specs=out_specs,
        out_shape=out_shape,
        compiler_params=pltpu.CompilerParams(dimension_semantics=("parallel", "parallel"),
                                             vmem_limit_bytes=VMEM_LIMIT),
        name="inproj",
    )(x, pos_col, pos_row, inv_row, inv_col, wn, wt)


def _gelu_tanh(x):
    return 0.5 * x * (1.0 + jnp.tanh(0.7978845608028654 * (x + 0.044715 * (x * x * x))))


def _compress_kernel(ck_ref, cv_ref, pe_ref, w1k_ref, w2k_ref, w1v_ref, w2vT_ref, kc_ref, vcT_ref):
    n_ch = ck_ref.shape[2]
    half = CMP_STRIDE * HEAD_DIM

    def hidden(c, pe_a, pe_b, w1_ref):
        a = (c + pe_a).astype(BF16)
        b = (c + pe_b).astype(BF16)
        u = _dot(a, w1_ref[0:half, :])
        v = _dot(b, w1_ref[half:2 * half, :])
        return _gelu_tanh(u + pltpu.roll(v, n_ch - 1, 0)).astype(BF16)

    for g in range(NSA_GROUPS):
        hk = hidden(ck_ref[0, g], pe_ref[0:1, :], pe_ref[1:2, :], w1k_ref)
        kc_ref[0, g] = _dot(hk, w2k_ref[...]).astype(BF16)
        hv = hidden(cv_ref[0, g], pe_ref[2:3, :], pe_ref[3:4, :], w1v_ref)
        vcT_ref[0, g] = _dot_nt(w2vT_ref[...], hv).astype(BF16)


def _compress(kc, vc, pos_emb, w_k1, w_k2, w_v1, w_v2):
    B, S, _ = kc.shape
    n_ch = S // CMP_STRIDE
    half = CMP_STRIDE * HEAD_DIM

    def chunks(a):
        a = a.reshape(B, n_ch, CMP_STRIDE, NSA_GROUPS, HEAD_DIM)
        return a.transpose(0, 3, 1, 2, 4).reshape(B, NSA_GROUPS, n_ch, half)

    pe = pos_emb.reshape(2, 2, half).reshape(4, half)
    const = lambda shp: pl.BlockSpec(shp, lambda b: (0,) * len(shp))
    return pl.pallas_call(
        _compress_kernel,
        grid=(B,),
        in_specs=[pl.BlockSpec((1, NSA_GROUPS, n_ch, half), lambda b: (b, 0, 0, 0)),
                  pl.BlockSpec((1, NSA_GROUPS, n_ch, half), lambda b: (b, 0, 0, 0)),
                  const((4, half)), const((2 * half, CMP_HIDDEN)), const((CMP_HIDDEN, HEAD_DIM)),
                  const((2 * half, CMP_HIDDEN)), const((HEAD_DIM, CMP_HIDDEN))],
        out_specs=(pl.BlockSpec((1, NSA_GROUPS, n_ch, HEAD_DIM), lambda b: (b, 0, 0, 0)),
                   pl.BlockSpec((1, NSA_GROUPS, HEAD_DIM, n_ch), lambda b: (b, 0, 0, 0))),
        out_shape=(jax.ShapeDtypeStruct((B, NSA_GROUPS, n_ch, HEAD_DIM), BF16),
                   jax.ShapeDtypeStruct((B, NSA_GROUPS, HEAD_DIM, n_ch), BF16)),
        compiler_params=pltpu.CompilerParams(dimension_semantics=("parallel",),
                                             vmem_limit_bytes=VMEM_LIMIT),
        name="compress",
    )(chunks(kc), chunks(vc), pe, w_k1.astype(BF16), w_k2.astype(BF16), w_v1.astype(BF16),
      w_v2.T.astype(BF16))


def _store_head_pairs(o_ref, heads):
    for pr in range(len(heads) // 2):
        pair = jnp.concatenate([heads[2 * pr], heads[2 * pr + 1]], axis=0)
        o_ref[0, :, LANES * pr:LANES * (pr + 1)] = pair.T.astype(BF16)


def _cmp_select_kernel(qT_ref, kc_ref, vcT_ref, gT_ref, o_ref, selb_ref, *, n_blk, n_sel):
    tq = qT_ref.shape[2]
    n_ch = kc_ref.shape[2]
    t = pl.program_id(1) * tq + lax.broadcasted_iota(jnp.int32, (1, tq), 1)
    cidx = lax.broadcasted_iota(jnp.int32, (n_ch, 1), 0)
    cmask = (cidx * CMP_STRIDE + (CMP_BLOCK - 1)) <= t
    jj = lax.broadcasted_iota(jnp.int32, (MAX_SEL_BLOCKS, n_ch), 0)
    cc = lax.broadcasted_iota(jnp.int32, (MAX_SEL_BLOCKS, n_ch), 1)
    ratio = SEL_BLOCK // CMP_STRIDE
    ov_t = jnp.where((cc >= ratio * jj - (CMP_BLOCK // CMP_STRIDE - 1)) & (cc <= ratio * jj + ratio - 1),
                     1.0, 0.0).astype(BF16)
    blk = lax.broadcasted_iota(jnp.int32, (MAX_SEL_BLOCKS, tq), 0)
    cur = t >> 6
    forced = (blk == 0) | (blk == cur) | (blk == cur - 1)

    heads = []
    for g in range(NSA_GROUPS):
        psum = jnp.zeros((n_ch, tq), F32)
        for r in range(NSA_REP):
            h = NSA_REP * g + r
            q_t = qT_ref[0, HEAD_DIM * h:HEAD_DIM * (h + 1), :]
            s = jnp.where(cmask, _dot(kc_ref[0, g], q_t), NEG_INF)
            m = jnp.max(s, axis=0, keepdims=True)
            e = jnp.where(cmask, jnp.exp2(s - m), 0.0)
            den = jnp.sum(e, axis=0, keepdims=True)
            p = e / jnp.maximum(den, TINY)
            o = _dot(vcT_ref[0, g], p.astype(BF16))
            heads.append(o * gT_ref[0, 3 * h:3 * h + 1, :])
            psum = psum + p
        hi = psum.astype(BF16)
        r1 = psum - hi.astype(F32)
        mid = r1.astype(BF16)
        lo = (r1 - mid.astype(F32)).astype(BF16)
        imp = _dot(ov_t, hi) + _dot(ov_t, mid) + _dot(ov_t, lo)
        imp = jnp.where(forced, FORCE, jnp.where(blk <= cur, imp, NEG_INF))
        rank = jnp.zeros((MAX_SEL_BLOCKS, tq), jnp.int32)
        for i in range(n_blk):
            row = imp[i:i + 1, :]
            rank = rank + jnp.where(blk > i, jnp.where(row >= imp, 1, 0), jnp.where(row > imp, 1, 0))
        selb_ref[0, g] = jnp.where(rank < n_sel, 0.0, NEG_INF).astype(BF16)
    _store_head_pairs(o_ref, heads)


def _cmp_select(qnnT, k_c, v_cT, gnT):
    B, _, S = qnnT.shape
    tq = ATTN_TQ
    n_ch = k_c.shape[2]
    n_blk = S // SEL_BLOCK
    assert n_blk <= MAX_SEL_BLOCKS
    kern = functools.partial(_cmp_select_kernel, n_blk=n_blk, n_sel=min(N_SEL, n_blk))
    return pl.pallas_call(
        kern,
        grid=(B, S // tq),
        in_specs=[pl.BlockSpec((1, NSA_Q, tq), lambda b, i: (b, 0, i)),
                  pl.BlockSpec((1, NSA_GROUPS, n_ch, HEAD_DIM), lambda b, i: (b, 0, 0, 0)),
                  pl.BlockSpec((1, NSA_GROUPS, HEAD_DIM, n_ch), lambda b, i: (b, 0, 0, 0)),
                  pl.BlockSpec((1, GATE_ROWS, tq), lambda b, i: (b, 0, i))],
        out_specs=(pl.BlockSpec((1, tq, NSA_Q), lambda b, i: (b, i, 0)),
                   pl.BlockSpec((1, NSA_GROUPS, MAX_SEL_BLOCKS, tq), lambda b, i: (b, 0, 0, i))),
        out_shape=(jax.ShapeDtypeStruct((B, S, NSA_Q), BF16),
                   jax.ShapeDtypeStruct((B, NSA_GROUPS, MAX_SEL_BLOCKS, S), BF16)),
        compiler_params=pltpu.CompilerParams(dimension_semantics=("parallel", "parallel"),
                                             vmem_limit_bytes=VMEM_LIMIT),
        name="cmp_select",
    )(qnnT, k_c, v_cT, gnT)


def _chunk_update(carry, s, v_aug, mask):
    m, acc = carry
    if mask is not None:
        s = jnp.where(mask, s, NEG_INF)
    m_new = jnp.maximum(m, jnp.max(s, axis=0, keepdims=True))
    p = jnp.exp2(s - m_new)
    if mask is not None:
        p = jnp.where(mask, p, 0.0)
    alpha = jnp.exp2(m - m_new)
    return m_new, alpha * acc + _dot(v_aug, p.astype(BF16))


def _attn_kernel(*refs, n_heads, rep, head_block, nprev, selected, use_sink, gate_branch):
    refs = list(refs)
    qT_ref, k_ref, vT_ref = refs[:3]
    pos = 3
    gT_ref = selb_ref = e_ref = sink_ref = None
    if gate_branch is not None:
        gT_ref = refs[pos]; pos += 1
    if selected:
        selb_ref, e_ref = refs[pos], refs[pos + 1]; pos += 2
    if use_sink:
        sink_ref = refs[pos]; pos += 1
    o_ref = refs[pos]

    tq = qT_ref.shape[2]
    i = pl.program_id(1)
    ksub = lax.broadcasted_iota(jnp.int32, (tq, tq), 0)
    qlane = lax.broadcasted_iota(jnp.int32, (tq, tq), 1)
    diag_mask = ksub <= qlane
    part_mask = ksub > qlane
    zeros_h = jnp.zeros((HEAD_DIM, tq), BF16)
    ones_rows = jnp.ones((DEN_ROWS, tq), BF16)
    acc_rows = HEAD_DIM + DEN_ROWS
    den_row = lax.broadcasted_iota(jnp.int32, (acc_rows, tq), 0) >= HEAD_DIM

    heads = []
    for hb in range(n_heads // head_block):
        hs = list(range(hb * head_block, (hb + 1) * head_block))
        groups = sorted({h // rep for h in hs})
        rhss = []
        for h in hs:
            g = h // rep
            q_t = qT_ref[0, HEAD_DIM * h:HEAD_DIM * (h + 1), :]
            parts = [zeros_h, zeros_h]
            parts[g] = q_t
            if selected:
                parts += [selb_ref[0, g], zeros_h]
            rhss.append(jnp.concatenate(parts, axis=0))

        def step(carries, c, mask, hs=hs, groups=groups, rhss=rhss):
            start = pl.multiple_of(c * tq, tq)
            k = k_ref[0, pl.ds(start, tq), :]
            if selected:
                k = jnp.concatenate([k, e_ref[pl.ds(start, tq), :]], axis=1)
            v_aug = {g: jnp.concatenate([vT_ref[0, c, HEAD_DIM * g:HEAD_DIM * (g + 1), :], ones_rows], axis=0)
                     for g in groups}
            scores = [_dot(k, rhs) for rhs in rhss]
            return tuple(_chunk_update(cr, s, v_aug[h // rep], mask)
                         for cr, s, h in zip(carries, scores, hs))

        if use_sink:
            carries = tuple((jnp.full((1, tq), sink_ref[h] * LOG2E, F32), jnp.where(den_row, 1.0, 0.0))
                            for h in hs)
        else:
            carries = tuple((jnp.full((1, tq), NEG_INF, F32), jnp.zeros((acc_rows, tq), F32)) for h in hs)
        if selected:
            carries = lax.fori_loop(0, i, lambda c, cr, step=step: step(cr, c, None), carries)
        else:
            for d in range(nprev, 0, -1):
                valid = i >= d
                mask = jnp.logical_and(part_mask, valid) if d == nprev else jnp.broadcast_to(valid, (tq, tq))
                carries = step(carries, jnp.maximum(i - d, 0), mask)
        carries = step(carries, i, diag_mask)
        for h, (m, acc) in zip(hs, carries):
            o = acc[0:HEAD_DIM] * (1.0 / jnp.maximum(acc[HEAD_DIM:HEAD_DIM + 1], TINY))
            if gate_branch is not None:
                o = o * gT_ref[0, 3 * h + gate_branch:3 * h + gate_branch + 1, :]
            heads.append(o)
    _store_head_pairs(o_ref, heads)


def _attention(qT, k, vT, *, tq, rep, nprev=0, selected=False, gnT=None, gate_branch=None,
               selb=None, sinks=None):
    B, F, S = qT.shape
    n_heads = F // HEAD_DIM
    n_kv = vT.shape[2]
    assert vT.shape == (B, S // tq, n_kv, tq)
    in_specs = [pl.BlockSpec((1, F, tq), lambda b, i: (b, 0, i)),
                pl.BlockSpec((1, S, LANES), lambda b, i: (b, 0, 0)),
                pl.BlockSpec((1, S // tq, n_kv, tq), lambda b, i: (b, 0, 0, 0))]
    args = [qT, k, vT]
    if gate_branch is not None:
        in_specs.append(pl.BlockSpec((1, GATE_ROWS, tq), lambda b, i: (b, 0, i)))
        args.append(gnT)
    if selected:
        onehot = (jnp.arange(S)[:, None] // SEL_BLOCK == jnp.arange(LANES)[None, :]).astype(BF16)
        in_specs += [pl.BlockSpec((1, NSA_GROUPS, MAX_SEL_BLOCKS, tq), lambda b, i: (b, 0, 0, i)),
                     pl.BlockSpec((S, LANES), lambda b, i: (0, 0))]
        args += [selb, onehot]
    if sinks is not None:
        in_specs.append(pl.BlockSpec(memory_space=pltpu.SMEM))
        args.append(sinks)
    kern = functools.partial(_attn_kernel, n_heads=n_heads, rep=rep, head_block=ATTN_HEAD_BLOCK,
                             nprev=nprev, selected=selected,
                             use_sink=sinks is not None, gate_branch=gate_branch)
    return pl.pallas_call(
        kern,
        grid=(B, S // tq),
        in_specs=in_specs,
        out_specs=pl.BlockSpec((1, tq, F), lambda b, i: (b, i, 0)),
        out_shape=jax.ShapeDtypeStruct((B, S, F), BF16),
        compiler_params=pltpu.CompilerParams(dimension_semantics=("parallel", "parallel"),
                                             vmem_limit_bytes=VMEM_LIMIT),
        name="attn_sel" if selected else ("attn_swa" if sinks is not None else "attn_win"),
    )(*args)


def _layer_norm(r, g, b):
    mu = jnp.mean(r, axis=-1, keepdims=True)
    d = r - mu
    var = jnp.mean(d * d, axis=-1, keepdims=True)
    return d * lax.rsqrt(var + LN_EPS) * g + b


def _outproj_kernel(x_ref, oa_ref, oc_ref, os_ref, ow_ref, gm_ref, wa_ref, wb_ref, wo_ref, g_ref, b_ref,
                    h_ref, *, alpha):
    o_b = (oc_ref[...].astype(F32) + os_ref[...].astype(F32) + ow_ref[...].astype(F32)).astype(BF16)
    y = (gm_ref[:, 0:D_MODEL].astype(F32) * _dot(oa_ref[...], wa_ref[...])
         + gm_ref[:, D_MODEL:2 * D_MODEL].astype(F32) * _dot(o_b, wb_ref[...]))
    mix = _dot(y.astype(BF16), wo_ref[...])
    h_ref[...] = _layer_norm(alpha * x_ref[...] + mix, g_ref[...], b_ref[...])


def _outproj(x2, o_a, o_c, o_s, o_w, gm, w_a, w_b, w_o, ln_g, ln_b, alpha):
    T, D = x2.shape
    tm = MLP_TM
    row = lambda w: pl.BlockSpec((tm, w), lambda i: (i, 0))
    const = lambda shp: pl.BlockSpec(shp, lambda i: (0, 0))
    return pl.pallas_call(
        functools.partial(_outproj_kernel, alpha=alpha),
        grid=(T // tm,),
        in_specs=[row(D), row(SWA_Q), row(NSA_Q), row(NSA_Q), row(NSA_Q), row(2 * D),
                  const((SWA_Q, D)), const((NSA_Q, D)), const((D, D)), const((1, D)), const((1, D))],
        out_specs=row(D),
        out_shape=jax.ShapeDtypeStruct((T, D), F32),
        compiler_params=pltpu.CompilerParams(dimension_semantics=("parallel",),
                                             vmem_limit_bytes=VMEM_LIMIT),
        name="outproj_ln1",
    )(x2, o_a, o_c, o_s, o_w, gm, w_a.astype(BF16), w_b.astype(BF16), w_o.astype(BF16),
      ln_g[None, :], ln_b[None, :])


def _ffn_kernel(h_ref, p_ref, wg_ref, wu_ref, wd_ref, wpg_ref, wple_ref, g_ref, b_ref, o_ref, *,
                alpha, ff_chunk):
    h = h_ref[...]
    hb = h.astype(BF16)
    ff = jnp.zeros(h.shape, F32)
    for c in range(D_FF // ff_chunk):
        sl = slice(c * ff_chunk, (c + 1) * ff_chunk)
        a = jax.nn.silu(_dot(hb, wg_ref[:, sl])) * _dot(hb, wu_ref[:, sl])
        ff = ff + _dot(a.astype(BF16), wd_ref[sl, :])
    ple = jax.nn.sigmoid(_dot(hb, wpg_ref[...])) * _dot(p_ref[...].astype(BF16), wple_ref[...])
    o_ref[...] = _layer_norm(alpha * h + ff + ple, g_ref[...], b_ref[...])


def _ffn(h1, p2, w_g, w_u, w_d, w_pg, w_ple, ln_g, ln_b, alpha):
    T, D = h1.shape
    tm = MLP_TM
    row = lambda w: pl.BlockSpec((tm, w), lambda i: (i, 0))
    const = lambda shp: pl.BlockSpec(shp, lambda i: (0, 0), pipeline_mode=pl.Buffered(1))
    return pl.pallas_call(
        functools.partial(_ffn_kernel, alpha=alpha, ff_chunk=D_FF // 2),
        grid=(T // tm,),
        in_specs=[row(D), row(PLE_DIM), const((D, D_FF)), const((D, D_FF)), const((D_FF, D)),
                  const((D, D)), const((PLE_DIM, D)), const((1, D)), const((1, D))],
        out_specs=row(D),
        out_shape=jax.ShapeDtypeStruct((T, D), F32),
        compiler_params=pltpu.CompilerParams(dimension_semantics=("parallel",),
                                             vmem_limit_bytes=VMEM_LIMIT),
        name="ffn_ple_ln2",
    )(h1, p2, w_g.astype(BF16), w_u.astype(BF16), w_d.astype(BF16), w_pg.astype(BF16),
      w_ple.astype(BF16), ln_g[None, :], ln_b[None, :])


def kernel(x, p, positions, w_in, attn_sinks, cmp_pos_emb, w_cmp_k1, w_cmp_k2, w_cmp_v1, w_cmp_v2, w_proj_swa, w_proj_nsa, w_out, ln1_g, ln1_b, w_ff_gate, w_ff_up, w_ff_down, w_ple, w_ple_gate, ln2_g, ln2_b):
    B, S, D = x.shape
    depth = w_in.shape[0]
    assert D == D_MODEL and S % INPROJ_TM == 0 and S >= NSA_WINDOW + ATTN_TQ
    alpha = (2.0 * depth) ** 0.25
    h = x
    for i in range(depth):
        (ks, kw, ka, kc, vc, gm, qaT, qnrT, qnnT, vsT, vwT, vaT, gnT) = _inproj(h, positions, w_in[i])
        k_c, v_cT = _compress(kc, vc, cmp_pos_emb[i], w_cmp_k1[i], w_cmp_k2[i], w_cmp_v1[i], w_cmp_v2[i])
        o_cmp, selb = _cmp_select(qnnT, k_c, v_cT, gnT)
        o_slc = _attention(qnrT, ks, vsT, tq=ATTN_TQ, rep=NSA_REP, selected=True, gnT=gnT, gate_branch=1,
                           selb=selb)
        o_win = _attention(qnrT, kw, vwT, tq=ATTN_TQ, rep=NSA_REP, nprev=NSA_WINDOW // ATTN_TQ, gnT=gnT,
                           gate_branch=2)
        o_swa = _attention(qaT, ka, vaT, tq=SWA_TQ, rep=SWA_HEADS, nprev=SWA_WINDOW // SWA_TQ,
                           sinks=attn_sinks[i].astype(F32))
        flat = lambda a: a.reshape(B * S, a.shape[-1])
        h1 = _outproj(flat(h), flat(o_swa), flat(o_cmp), flat(o_slc), flat(o_win), flat(gm),
                      w_proj_swa[i], w_proj_nsa[i], w_out[i], ln1_g[i], ln1_b[i], alpha)
        h2 = _ffn(h1, flat(p[i]), w_ff_gate[i], w_ff_up[i], w_ff_down[i], w_ple_gate[i], w_ple[i],
                  ln2_g[i], ln2_b[i], alpha)
        h = h2.reshape(B, S, D)
    return h
```

```python
import functools

import jax
import jax.numpy as jnp
from jax import lax
from jax.experimental import pallas as pl
from jax.experimental.pallas import tpu as pltpu

F32 = jnp.float32
BF16 = jnp.bfloat16

D_MODEL = 1024
HEAD_DIM = 64
HALF_DIM = HEAD_DIM // 2
ROPE_THETA = 10000.0
LN_EPS = 1e-5
NEG_INF = -1e30
FORCE = 1e30
TINY = 1e-30
PLE_DIM = 256

SWA_HEADS = 8
SWA_WINDOW = 128
NSA_HEADS = 8
NSA_GROUPS = 2
NSA_REP = NSA_HEADS // NSA_GROUPS
NSA_WINDOW = 512
CMP_BLOCK = 32
CMP_STRIDE = 16
CMP_HIDDEN = 256
SEL_BLOCK = 64
N_SEL = 16
D_FF = 2816

SWA_Q = SWA_HEADS * HEAD_DIM
NSA_Q = NSA_HEADS * HEAD_DIM
NSA_KV = NSA_GROUPS * HEAD_DIM
NSA_GATES = NSA_HEADS * 3
GATE_ROWS = 32
MAX_SEL_BLOCKS = 64
DEN_ROWS = 16
LOG2E = 1.4426950408889634

LANES = 128
VMEM_LIMIT = 56 * 1024 * 1024

TOK_K_COLS = 5 * LANES
TOK_COLS = TOK_K_COLS + 2 * D_MODEL
FEAT_V_ROWS = 2 * NSA_KV + HEAD_DIM + GATE_ROWS + 32
FEAT_ROWS = SWA_Q + NSA_Q + FEAT_V_ROWS

INPROJ_TM = 512
ATTN_TQ = 256
SWA_TQ = 128
SWA_SUBTILES = 2
MLP_TM = 512
FF_CHUNKS = ((0, 1536), (1536, D_FF))
MASK_VAL = 2.0 * NEG_INF
SUBLANES = 8


def _dot(a, b):
    return jnp.dot(a, b, preferred_element_type=F32)


def _dot_nt(a, b):
    return lax.dot_general(a, b, (((1,), (1,)), ((), ())), preferred_element_type=F32)


def _inproj_kernel(x_ref, posc_ref, posr_ref, invr_ref, invc_ref, wn_ref, wt_ref,
                   ks_ref, kw_ref, ka_ref, kc_ref, vc_ref, gm_ref,
                   qaT_ref, qnrT_ref, qnnT_ref, vsT_ref, vwT_ref, vaT_ref, gnT_ref):
    tm = x_ref.shape[1]
    xb = x_ref[0].astype(BF16)

    ang = posc_ref[0] * invr_ref[...]
    cos_t = jnp.cos(ang)
    sin_t = jnp.sin(ang)
    lane = lax.broadcasted_iota(jnp.int32, (tm, LANES), 1)
    first = (lane & (HEAD_DIM - 1)) < HALF_DIM
    sin_s = jnp.where(first, -sin_t, sin_t)

    def rope_tok(z):
        partner = jnp.where(first, pltpu.roll(z, LANES - HALF_DIM, 1), pltpu.roll(z, HALF_DIM, 1))
        return z * cos_t + partner * sin_s

    zk = _dot(xb, wn_ref[:, 0:TOK_K_COLS])
    ks_ref[0] = rope_tok(zk[:, 0:128]).astype(BF16)
    kw_ref[0] = rope_tok(zk[:, 128:256]).astype(BF16)
    ka_ref[0] = rope_tok(zk[:, 256:384]).astype(BF16)
    kc_ref[0] = zk[:, 384:512]
    vc_ref[0] = zk[:, 512:640]
    gchunk = 512
    for c in range(2 * D_MODEL // gchunk):
        zg = _dot(xb, wn_ref[:, TOK_K_COLS + c * gchunk:TOK_K_COLS + (c + 1) * gchunk])
        gm_ref[0, :, c * gchunk:(c + 1) * gchunk] = jax.nn.sigmoid(zg).astype(BF16)

    ang_t = invc_ref[...] * posr_ref[0]
    cos_f = jnp.cos(ang_t)
    sin_f = jnp.sin(ang_t)
    scale = HEAD_DIM ** -0.5 * LOG2E

    def rope_feat_store(z, out_ref, n_heads):
        for h in range(n_heads):
            x1 = z[HEAD_DIM * h:HEAD_DIM * h + HALF_DIM]
            x2 = z[HEAD_DIM * h + HALF_DIM:HEAD_DIM * (h + 1)]
            out_ref[0, HEAD_DIM * h:HEAD_DIM * h + HALF_DIM, :] = ((x1 * cos_f - x2 * sin_f) * scale).astype(BF16)
            out_ref[0, HEAD_DIM * h + HALF_DIM:HEAD_DIM * (h + 1), :] = ((x2 * cos_f + x1 * sin_f) * scale).astype(BF16)

    zqa = _dot_nt(wt_ref[0:SWA_Q, :], xb)
    rope_feat_store(zqa, qaT_ref, SWA_HEADS)
    zqn = _dot_nt(wt_ref[SWA_Q:SWA_Q + NSA_Q, :], xb)
    rope_feat_store(zqn, qnrT_ref, NSA_HEADS)
    qnnT_ref[0] = (zqn * scale).astype(BF16)
    zv = _dot_nt(wt_ref[SWA_Q + NSA_Q:FEAT_ROWS, :], xb)
    for j in range(tm // ATTN_TQ):
        sl = slice(j * ATTN_TQ, (j + 1) * ATTN_TQ)
        vsT_ref[0, j] = zv[0:NSA_KV, sl].astype(BF16)
        vwT_ref[0, j] = zv[NSA_KV:2 * NSA_KV, sl].astype(BF16)
    for j in range(tm // SWA_TQ):
        sl = slice(j * SWA_TQ, (j + 1) * SWA_TQ)
        vaT_ref[0, j] = zv[2 * NSA_KV:2 * NSA_KV + HEAD_DIM, sl].astype(BF16)
    g0 = 2 * NSA_KV + HEAD_DIM
    gnT_ref[0] = jax.nn.sigmoid(zv[g0:g0 + GATE_ROWS])


def _inproj(x, positions, w_in):
    B, S, D = x.shape
    tm = INPROJ_TM
    offs = {}
    off = 0
    for name, n in (("qa", SWA_Q), ("ka", HEAD_DIM), ("va", HEAD_DIM), ("qn", NSA_Q), ("kc", NSA_KV),
                    ("vc", NSA_KV), ("ks", NSA_KV), ("vs", NSA_KV), ("kw", NSA_KV), ("vw", NSA_KV),
                    ("gn", NSA_GATES), ("gm", 2 * D_MODEL)):
        offs[name] = (off, off + n)
        off += n
    col = lambda name: w_in[:, offs[name][0]:offs[name][1]]
    zpad = lambda n: jnp.zeros((D, n), w_in.dtype)
    wn = jnp.concatenate([col("ks"), col("kw"), col("ka"), zpad(LANES - HEAD_DIM), col("kc"), col("vc"),
                          col("gm")], axis=1).astype(BF16)
    wt = jnp.concatenate([col("qa"), col("qn"), col("vs"), col("vw"), col("va"), col("gn"),
                          zpad(FEAT_ROWS - (SWA_Q + NSA_Q + 2 * NSA_KV + HEAD_DIM + NSA_GATES))],
                         axis=1).T.astype(BF16)
    assert wn.shape == (D, TOK_COLS) and wt.shape == (FEAT_ROWS, D)

    inv = ROPE_THETA ** (-jnp.arange(0, HEAD_DIM, 2, dtype=F32) / HEAD_DIM)
    inv_row = jnp.tile(inv, LANES // HALF_DIM)[None, :]
    inv_col = inv[:, None]
    posf = positions.astype(F32)
    pos_col = posf[:, :, None]
    pos_row = posf[:, None, :]

    tok = lambda w: pl.BlockSpec((1, tm, w), lambda b, i: (b, i, 0))
    feat = lambda r: pl.BlockSpec((1, r, tm), lambda b, i: (b, 0, i))
    const = lambda shp: pl.BlockSpec(shp, lambda b, i: (0,) * len(shp))
    out_shape = (
        jax.ShapeDtypeStruct((B, S, LANES), BF16),
        jax.ShapeDtypeStruct((B, S, LANES), BF16),
        jax.ShapeDtypeStruct((B, S, LANES), BF16),
        jax.ShapeDtypeStruct((B, S, LANES), F32),
        jax.ShapeDtypeStruct((B, S, LANES), F32),
        jax.ShapeDtypeStruct((B, S, 2 * D_MODEL), BF16),
        jax.ShapeDtypeStruct((B, SWA_Q, S), BF16),
        jax.ShapeDtypeStruct((B, NSA_Q, S), BF16),
        jax.ShapeDtypeStruct((B, NSA_Q, S), BF16),
        jax.ShapeDtypeStruct((B, S // ATTN_TQ, NSA_KV, ATTN_TQ), BF16),
        jax.ShapeDtypeStruct((B, S // ATTN_TQ, NSA_KV, ATTN_TQ), BF16),
        jax.ShapeDtypeStruct((B, S // SWA_TQ, HEAD_DIM, SWA_TQ), BF16),
        jax.ShapeDtypeStruct((B, GATE_ROWS, S), F32),
    )
    out_specs = (
        tok(LANES), tok(LANES), tok(LANES), tok(LANES), tok(LANES), tok(2 * D_MODEL),
        feat(SWA_Q), feat(NSA_Q), feat(NSA_Q),
        pl.BlockSpec((1, tm // ATTN_TQ, NSA_KV, ATTN_TQ), lambda b, i: (b, i, 0, 0)),
        pl.BlockSpec((1, tm // ATTN_TQ, NSA_KV, ATTN_TQ), lambda b, i: (b, i, 0, 0)),
        pl.BlockSpec((1, tm // SWA_TQ, HEAD_DIM, SWA_TQ), lambda b, i: (b, i, 0, 0)),
        feat(GATE_ROWS),
    )
    return pl.pallas_call(
        _inproj_kernel,
        grid=(B, S // tm),
        in_specs=[tok(D), tok(1), pl.BlockSpec((1, 1, tm), lambda b, i: (b, 0, i)),
                  const((1, LANES)), const((HALF_DIM, 1)), const((D, TOK_COLS)), const((FEAT_ROWS, D))],
        out_specs=out_specs,
        out_shape=out_shape,
        compiler_params=pltpu.CompilerParams(dimension_semantics=("parallel", "parallel"),
                                             vmem_limit_bytes=VMEM_LIMIT),
        name="inproj",
    )(x, pos_col, pos_row, inv_row, inv_col, wn, wt)


def _gelu_tanh(x):
    return 0.5 * x * (1.0 + jnp.tanh(0.7978845608028654 * (x + 0.044715 * (x * x * x))))


def _compress_kernel(kc_ref, vc_ref, pe_ref, wbk_ref, w1k_ref, w2k_ref, wbv_ref, w1v_ref, w2vT_ref,
                     kcmp_ref, vcmpT_ref):
    n_ch = kc_ref.shape[1] // CMP_STRIDE

    def hidden(x_ref, wb_ref, w1_ref, pe_row):
        acc = jnp.zeros((n_ch, 2 * NSA_GROUPS * CMP_HIDDEN), F32)
        for t in range(CMP_STRIDE):
            x_t = x_ref[0, pl.ds(t, n_ch, stride=CMP_STRIDE), :].astype(BF16)
            acc = acc + _dot(x_t, wb_ref[t])
        bias = _dot(jnp.broadcast_to(pe_row, (SUBLANES, pe_row.shape[1])).astype(BF16), w1_ref[...])[0:1]
        out = []
        for g in range(NSA_GROUPS):
            u = acc[:, 2 * CMP_HIDDEN * g:2 * CMP_HIDDEN * g + CMP_HIDDEN]
            v = acc[:, 2 * CMP_HIDDEN * g + CMP_HIDDEN:2 * CMP_HIDDEN * (g + 1)]
            out.append(_gelu_tanh(u + pltpu.roll(v, n_ch - 1, 0) + bias).astype(BF16))
        return out

    hk = hidden(kc_ref, wbk_ref, w1k_ref, pe_ref[0:1, :])
    hv = hidden(vc_ref, wbv_ref, w1v_ref, pe_ref[1:2, :])
    for g in range(NSA_GROUPS):
        kcmp_ref[0, g] = _dot(hk[g], w2k_ref[...]).astype(BF16)
        vcmpT_ref[0, g] = _dot_nt(w2vT_ref[...], hv[g]).astype(BF16)


def _compress(kc, vc, pos_emb, w_k1, w_k2, w_v1, w_v2):
    B, S, _ = kc.shape
    n_ch = S // CMP_STRIDE
    half = CMP_STRIDE * HEAD_DIM

    def per_token(w1):
        blk = jnp.concatenate([w1[:half].reshape(CMP_STRIDE, HEAD_DIM, CMP_HIDDEN),
                               w1[half:].reshape(CMP_STRIDE, HEAD_DIM, CMP_HIDDEN)], axis=2)
        z = jnp.zeros_like(blk)
        return jnp.concatenate([jnp.concatenate([blk, z], axis=2), jnp.concatenate([z, blk], axis=2)],
                               axis=1).astype(BF16)

    pe = pos_emb.reshape(2, 2 * half)
    const = lambda shp: pl.BlockSpec(shp, lambda b: (0,) * len(shp))
    wb_shape = (CMP_STRIDE, NSA_KV, 2 * NSA_GROUPS * CMP_HIDDEN)
    return pl.pallas_call(
        _compress_kernel,
        grid=(B,),
        in_specs=[pl.BlockSpec((1, S, NSA_KV), lambda b: (b, 0, 0)),
                  pl.BlockSpec((1, S, NSA_KV), lambda b: (b, 0, 0)),
                  const((2, 2 * half)),
                  const(wb_shape), const((2 * half, CMP_HIDDEN)), const((CMP_HIDDEN, HEAD_DIM)),
                  const(wb_shape), const((2 * half, CMP_HIDDEN)), const((HEAD_DIM, CMP_HIDDEN))],
        out_specs=(pl.BlockSpec((1, NSA_GROUPS, n_ch, HEAD_DIM), lambda b: (b, 0, 0, 0)),
                   pl.BlockSpec((1, NSA_GROUPS, HEAD_DIM, n_ch), lambda b: (b, 0, 0, 0))),
        out_shape=(jax.ShapeDtypeStruct((B, NSA_GROUPS, n_ch, HEAD_DIM), BF16),
                   jax.ShapeDtypeStruct((B, NSA_GROUPS, HEAD_DIM, n_ch), BF16)),
        compiler_params=pltpu.CompilerParams(dimension_semantics=("parallel",),
                                             vmem_limit_bytes=VMEM_LIMIT),
        name="compress",
    )(kc, vc, pe, per_token(w_k1), w_k1.astype(BF16), w_k2.astype(BF16),
      per_token(w_v1), w_v1.astype(BF16), w_v2.T.astype(BF16))


def _store_head_pairs(o_ref, heads, rows=slice(None)):
    for pr in range(len(heads) // 2):
        pair = jnp.concatenate([heads[2 * pr], heads[2 * pr + 1]], axis=0)
        o_ref[0, rows, LANES * pr:LANES * (pr + 1)] = pair.T.astype(BF16)


def _cmp_select_kernel(qT_ref, kc_ref, vcT_ref, gT_ref, o_ref, selb_ref, s_ref, psum_ref, imp_ref, rank_ref, *,
                       n_blk, n_sel):
    tq = qT_ref.shape[2]
    n_ch = kc_ref.shape[2]
    t = pl.program_id(1) * tq + lax.broadcasted_iota(jnp.int32, (1, tq), 1)
    cidx = lax.broadcasted_iota(jnp.int32, (n_ch, 1), 0)
    cmask = (cidx * CMP_STRIDE + (CMP_BLOCK - 1)) <= t
    blk = lax.broadcasted_iota(jnp.int32, (n_blk, tq), 0)
    cur = t >> 6
    forced = (blk == 0) | (blk == cur) | (blk == cur - 1)
    ratio = SEL_BLOCK // CMP_STRIDE

    for h in range(NSA_HEADS):
        s_ref[h] = _dot(kc_ref[0, h // NSA_REP], qT_ref[0, HEAD_DIM * h:HEAD_DIM * (h + 1), :])
    heads = []
    for g in range(NSA_GROUPS):
        psum = jnp.zeros((n_ch, tq), F32)
        for r in range(NSA_REP):
            h = NSA_REP * g + r
            s = jnp.where(cmask, s_ref[h], MASK_VAL)
            m = jnp.maximum(jnp.max(s, axis=0, keepdims=True), NEG_INF)
            e = jnp.exp2(s - m)
            den = jnp.sum(e, axis=0, keepdims=True)
            p = e * (1.0 / jnp.maximum(den, TINY))
            o = _dot(vcT_ref[0, g], p.astype(BF16))
            heads.append(o * gT_ref[0, 3 * h:3 * h + 1, :])
            psum = psum + p
        for j in range(tq // LANES):
            psum_ref[j] = psum[:, LANES * j:LANES * (j + 1)]
        rows = [jnp.concatenate([psum_ref[j, pl.ds(k, n_blk, stride=ratio), :] for j in range(tq // LANES)],
                                axis=1) for k in range(ratio)]
        prev = jnp.where(blk == 0, 0.0, pltpu.roll(rows[ratio - 1], 1, 0))
        imp = prev + rows[0] + rows[1] + rows[2] + rows[3]
        imp_ref[g, 0:n_blk, :] = jnp.where(forced, FORCE, jnp.where(blk <= cur, imp, NEG_INF))
        if n_blk < MAX_SEL_BLOCKS:
            imp_ref[g, n_blk:MAX_SEL_BLOCKS, :] = jnp.full((MAX_SEL_BLOCKS - n_blk, tq), NEG_INF, F32)
        rank_ref[g] = jnp.zeros((MAX_SEL_BLOCKS, tq), jnp.int32)
    _store_head_pairs(o_ref, heads)

    last_cur = (pl.program_id(1) * tq + tq - 1) >> 6
    sub = lax.broadcasted_iota(jnp.int32, (SUBLANES, tq), 0)
    n_groups = -(-n_blk // SUBLANES)
    for rg in range(n_groups):
        @pl.when(rg * SUBLANES <= last_cur)
        def _(rg=rg):
            for g in range(NSA_GROUPS):
                src = imp_ref[g, SUBLANES * rg:SUBLANES * (rg + 1), :]
                for v in range(n_groups):
                    tgt = imp_ref[g, SUBLANES * v:SUBLANES * (v + 1), :]
                    cnt = jnp.zeros((SUBLANES, tq), jnp.int32)
                    for r in range(SUBLANES):
                        row = src[r:r + 1, :]
                        if v > rg:
                            cnt = cnt + jnp.where(row >= tgt, 1, 0)
                        elif v < rg:
                            cnt = cnt + jnp.where(row > tgt, 1, 0)
                        else:
                            cnt = cnt + jnp.where(sub > r, jnp.where(row >= tgt, 1, 0), jnp.where(row > tgt, 1, 0))
                    rank_ref[g, SUBLANES * v:SUBLANES * (v + 1), :] += cnt
    for g in range(NSA_GROUPS):
        selb_ref[0, g] = jnp.where(rank_ref[g] < n_sel, 0.0, MASK_VAL).astype(BF16)


def _cmp_select(qnnT, k_c, v_cT, gnT):
    B, _, S = qnnT.shape
    tq = ATTN_TQ
    n_ch = k_c.shape[2]
    n_blk = S // SEL_BLOCK
    assert n_blk <= MAX_SEL_BLOCKS
    kern = functools.partial(_cmp_select_kernel, n_blk=n_blk, n_sel=min(N_SEL, n_blk))
    return pl.pallas_call(
        kern,
        grid=(B, S // tq),
        in_specs=[pl.BlockSpec((1, NSA_Q, tq), lambda b, i: (b, 0, i)),
                  pl.BlockSpec((1, NSA_GROUPS, n_ch, HEAD_DIM), lambda b, i: (b, 0, 0, 0)),
                  pl.BlockSpec((1, NSA_GROUPS, HEAD_DIM, n_ch), lambda b, i: (b, 0, 0, 0)),
                  pl.BlockSpec((1, GATE_ROWS, tq), lambda b, i: (b, 0, i))],
        out_specs=(pl.BlockSpec((1, tq, NSA_Q), lambda b, i: (b, i, 0)),
                   pl.BlockSpec((1, NSA_GROUPS, MAX_SEL_BLOCKS, tq), lambda b, i: (b, 0, 0, i))),
        out_shape=(jax.ShapeDtypeStruct((B, S, NSA_Q), BF16),
                   jax.ShapeDtypeStruct((B, NSA_GROUPS, MAX_SEL_BLOCKS, S), BF16)),
        scratch_shapes=[pltpu.VMEM((NSA_HEADS, n_ch, tq), F32),
                        pltpu.VMEM((tq // LANES, n_ch, LANES), F32),
                        pltpu.VMEM((NSA_GROUPS, MAX_SEL_BLOCKS, tq), F32),
                        pltpu.VMEM((NSA_GROUPS, MAX_SEL_BLOCKS, tq), jnp.int32)],
        compiler_params=pltpu.CompilerParams(dimension_semantics=("parallel", "parallel"),
                                             vmem_limit_bytes=VMEM_LIMIT),
        name="cmp_select",
    )(qnnT, k_c, v_cT, gnT)


def _attn_kernel(*refs, n_heads, rep, n_sub, nprev, selected, use_sink, gate_branch):
    refs = list(refs)
    qT_ref, k_ref, vT_ref = refs[:3]
    pos = 3
    gT_ref = selb_ref = e_ref = sink_ref = None
    if gate_branch is not None:
        gT_ref = refs[pos]; pos += 1
    if selected:
        selb_ref, e_ref = refs[pos], refs[pos + 1]; pos += 2
    if use_sink:
        sink_ref = refs[pos]; pos += 1
    o_ref = refs[pos]
    rhs_ref, sa_ref, sb_ref, acc_ref, m_ref = refs[pos + 1:pos + 6]

    tq = sa_ref.shape[1]
    ksub = lax.broadcasted_iota(jnp.int32, (tq, tq), 0)
    qlane = lax.broadcasted_iota(jnp.int32, (tq, tq), 1)
    zeros_h = jnp.zeros((HEAD_DIM, tq), BF16)
    ones_rows = jnp.ones((DEN_ROWS, tq), BF16)
    acc_rows = HEAD_DIM + DEN_ROWS
    den_row = lax.broadcasted_iota(jnp.int32, (acc_rows, tq), 0) >= HEAD_DIM
    groups = sorted({h // rep for h in range(n_heads)})

    def qk(c, dst_ref):
        start = pl.multiple_of(c * tq, tq)
        k = k_ref[0, pl.ds(start, tq), :]
        if selected:
            k = jnp.concatenate([k, e_ref[pl.ds(start, tq), :]], axis=1)
        for h in range(n_heads):
            dst_ref[h] = _dot(k, rhs_ref[h])

    def softmax_pv(c, src_ref, mask=None, bias=None):
        v_aug = {g: jnp.concatenate([vT_ref[0, c, HEAD_DIM * g:HEAD_DIM * (g + 1), :], ones_rows], axis=0)
                 for g in groups}
        for h in range(n_heads):
            s = src_ref[h]
            if bias is not None:
                s = s + bias
            if mask is not None:
                s = jnp.where(mask, s, MASK_VAL)
            m_old = m_ref[h]
            m_new = jnp.maximum(m_old, jnp.max(s, axis=0, keepdims=True))
            p = jnp.exp2((s - m_new).astype(BF16))
            acc_ref[h] = jnp.exp2(m_old - m_new) * acc_ref[h] + _dot(v_aug[h // rep], p)
            m_ref[h] = m_new

    for sub in range(n_sub):
        i = pl.program_id(1) * n_sub + sub
        cols = slice(sub * tq, (sub + 1) * tq)
        for h in range(n_heads):
            g = h // rep
            parts = [zeros_h, zeros_h]
            parts[g] = qT_ref[0, HEAD_DIM * h:HEAD_DIM * (h + 1), cols]
            if selected:
                parts += [selb_ref[0, g, :, cols], zeros_h]
            rhs_ref[h] = jnp.concatenate(parts, axis=0)
            if use_sink:
                m_ref[h] = jnp.full((1, tq), sink_ref[h] * LOG2E, F32)
                acc_ref[h] = jnp.where(den_row, 1.0, 0.0)
            else:
                m_ref[h] = jnp.full((1, tq), NEG_INF, F32)
                acc_ref[h] = jnp.zeros((acc_rows, tq), F32)

        if selected:
            qk(0, sa_ref)

            def pair(j, carry):
                c = 2 * j
                qk(c + 1, sb_ref)
                softmax_pv(c, sa_ref)
                qk(c + 2, sa_ref)
                softmax_pv(c + 1, sb_ref)
                return carry

            lax.fori_loop(0, i // 2, pair, 0)
            odd = i % 2
            last = i - odd

            @pl.when(odd == 1)
            def _():
                qk(i, sb_ref)

            softmax_pv(last, sa_ref, mask=ksub <= qlane + odd * tq)

            @pl.when(odd == 1)
            def _():
                softmax_pv(i, sb_ref, mask=ksub <= qlane)
        else:
            bufs = (sa_ref, sb_ref)
            chunks = [jnp.maximum(i - d, 0) for d in range(nprev, 0, -1)] + [i]
            qk(chunks[0], bufs[0])
            for n, d in enumerate(range(nprev, -1, -1)):
                if n + 1 < len(chunks):
                    qk(chunks[n + 1], bufs[(n + 1) % 2])
                if d == 0:
                    softmax_pv(chunks[n], bufs[n % 2], mask=ksub <= qlane)
                elif d == nprev:
                    gone = jnp.where(i >= d, 0, tq)
                    softmax_pv(chunks[n], bufs[n % 2], mask=ksub > qlane + gone)
                else:
                    softmax_pv(chunks[n], bufs[n % 2], bias=jnp.where(i >= d, 0.0, MASK_VAL))

        heads = []
        for h in range(n_heads):
            acc = acc_ref[h]
            o = acc[0:HEAD_DIM] * (1.0 / jnp.maximum(acc[HEAD_DIM:HEAD_DIM + 1], TINY))
            if gate_branch is not None:
                o = o * gT_ref[0, 3 * h + gate_branch:3 * h + gate_branch + 1, cols]
            heads.append(o)
        _store_head_pairs(o_ref, heads, cols)


def _attention(qT, k, vT, *, tq, rep, n_sub=1, nprev=0, selected=False, gnT=None, gate_branch=None,
               selb=None, sinks=None):
    B, F, S = qT.shape
    n_heads = F // HEAD_DIM
    n_kv = vT.shape[2]
    tstep = tq * n_sub
    assert vT.shape == (B, S // tq, n_kv, tq) and S % tstep == 0
    in_specs = [pl.BlockSpec((1, F, tstep), lambda b, i: (b, 0, i)),
                pl.BlockSpec((1, S, LANES), lambda b, i: (b, 0, 0)),
                pl.BlockSpec((1, S // tq, n_kv, tq), lambda b, i: (b, 0, 0, 0))]
    args = [qT, k, vT]
    if gate_branch is not None:
        in_specs.append(pl.BlockSpec((1, GATE_ROWS, tstep), lambda b, i: (b, 0, i)))
        args.append(gnT)
    if selected:
        onehot = (jnp.arange(S)[:, None] // SEL_BLOCK == jnp.arange(LANES)[None, :]).astype(BF16)
        in_specs += [pl.BlockSpec((1, NSA_GROUPS, MAX_SEL_BLOCKS, tstep), lambda b, i: (b, 0, 0, i)),
                     pl.BlockSpec((S, LANES), lambda b, i: (0, 0))]
        args += [selb, onehot]
    if sinks is not None:
        in_specs.append(pl.BlockSpec(memory_space=pltpu.SMEM))
        args.append(sinks)
    kern = functools.partial(_attn_kernel, n_heads=n_heads, rep=rep, n_sub=n_sub, nprev=nprev,
                             selected=selected, use_sink=sinks is not None, gate_branch=gate_branch)
    contraction = 2 * LANES if selected else LANES
    return pl.pallas_call(
        kern,
        grid=(B, S // tstep),
        in_specs=in_specs,
        out_specs=pl.BlockSpec((1, tstep, F), lambda b, i: (b, i, 0)),
        out_shape=jax.ShapeDtypeStruct((B, S, F), BF16),
        scratch_shapes=[pltpu.VMEM((n_heads, contraction, tq), BF16),
                        pltpu.VMEM((n_heads, tq, tq), F32),
                        pltpu.VMEM((n_heads, tq, tq), F32),
                        pltpu.VMEM((n_heads, HEAD_DIM + DEN_ROWS, tq), F32),
                        pltpu.VMEM((n_heads, 1, tq), F32)],
        compiler_params=pltpu.CompilerParams(dimension_semantics=("parallel", "parallel"),
                                             vmem_limit_bytes=VMEM_LIMIT),
        name="attn_sel" if selected else ("attn_swa" if sinks is not None else "attn_win"),
    )(*args)


def _layer_norm(r, g, b):
    mu = jnp.mean(r, axis=-1, keepdims=True)
    d = r - mu
    var = jnp.mean(d * d, axis=-1, keepdims=True)
    return d * lax.rsqrt(var + LN_EPS) * g + b


def _outproj_kernel(x_ref, oa_ref, oc_ref, os_ref, ow_ref, gm_ref, wa_ref, wb_ref, wo_ref, g_ref, b_ref,
                    h_ref, *, alpha):
    o_b = (oc_ref[...].astype(F32) + os_ref[...].astype(F32) + ow_ref[...].astype(F32)).astype(BF16)
    y = (gm_ref[:, 0:D_MODEL].astype(F32) * _dot(oa_ref[...], wa_ref[...])
         + gm_ref[:, D_MODEL:2 * D_MODEL].astype(F32) * _dot(o_b, wb_ref[...]))
    mix = _dot(y.astype(BF16), wo_ref[...])
    h_ref[...] = _layer_norm(alpha * x_ref[...] + mix, g_ref[...], b_ref[...])


def _outproj(x2, o_a, o_c, o_s, o_w, gm, w_a, w_b, w_o, ln_g, ln_b, alpha):
    T, D = x2.shape
    tm = MLP_TM
    row = lambda w: pl.BlockSpec((tm, w), lambda i: (i, 0))
    const = lambda shp: pl.BlockSpec(shp, lambda i: (0, 0))
    return pl.pallas_call(
        functools.partial(_outproj_kernel, alpha=alpha),
        grid=(T // tm,),
        in_specs=[row(D), row(SWA_Q), row(NSA_Q), row(NSA_Q), row(NSA_Q), row(2 * D),
                  const((SWA_Q, D)), const((NSA_Q, D)), const((D, D)), const((1, D)), const((1, D))],
        out_specs=row(D),
        out_shape=jax.ShapeDtypeStruct((T, D), F32),
        compiler_params=pltpu.CompilerParams(dimension_semantics=("parallel",),
                                             vmem_limit_bytes=VMEM_LIMIT),
        name="outproj_ln1",
    )(x2, o_a, o_c, o_s, o_w, gm, w_a.astype(BF16), w_b.astype(BF16), w_o.astype(BF16),
      ln_g[None, :], ln_b[None, :])


def _ffn_kernel(h_ref, p_ref, wg_ref, wu_ref, wd_ref, wpg_ref, wple_ref, g_ref, b_ref, o_ref, *,
                alpha):
    h = h_ref[...]
    hb = h.astype(BF16)
    ff = jnp.zeros(h.shape, F32)
    for lo, hi in FF_CHUNKS:
        sl = slice(lo, hi)
        a = jax.nn.silu(_dot(hb, wg_ref[:, sl])) * _dot(hb, wu_ref[:, sl])
        ff = ff + _dot(a.astype(BF16), wd_ref[sl, :])
    ple = jax.nn.sigmoid(_dot(hb, wpg_ref[...])) * _dot(p_ref[...].astype(BF16), wple_ref[...])
    o_ref[...] = _layer_norm(alpha * h + ff + ple, g_ref[...], b_ref[...])


def _ffn(h1, p2, w_g, w_u, w_d, w_pg, w_ple, ln_g, ln_b, alpha):
    T, D = h1.shape
    tm = MLP_TM
    row = lambda w: pl.BlockSpec((tm, w), lambda i: (i, 0))
    const = lambda shp: pl.BlockSpec(shp, lambda i: (0, 0), pipeline_mode=pl.Buffered(1))
    return pl.pallas_call(
        functools.partial(_ffn_kernel, alpha=alpha),
        grid=(T // tm,),
        in_specs=[row(D), row(PLE_DIM), const((D, D_FF)), const((D, D_FF)), const((D_FF, D)),
                  const((D, D)), const((PLE_DIM, D)), const((1, D)), const((1, D))],
        out_specs=row(D),
        out_shape=jax.ShapeDtypeStruct((T, D), F32),
        compiler_params=pltpu.CompilerParams(dimension_semantics=("parallel",),
                                             vmem_limit_bytes=VMEM_LIMIT),
        name="ffn_ple_ln2",
    )(h1, p2, w_g.astype(BF16), w_u.astype(BF16), w_d.astype(BF16), w_pg.astype(BF16),
      w_ple.astype(BF16), ln_g[None, :], ln_b[None, :])


def kernel(x, p, positions, w_in, attn_sinks, cmp_pos_emb, w_cmp_k1, w_cmp_k2, w_cmp_v1, w_cmp_v2, w_proj_swa, w_proj_nsa, w_out, ln1_g, ln1_b, w_ff_gate, w_ff_up, w_ff_down, w_ple, w_ple_gate, ln2_g, ln2_b):
    B, S, D = x.shape
    depth = w_in.shape[0]
    assert D == D_MODEL and S % INPROJ_TM == 0 and S >= NSA_WINDOW + ATTN_TQ
    alpha = (2.0 * depth) ** 0.25
    h = x
    for i in range(depth):
        (ks, kw, ka, kc, vc, gm, qaT, qnrT, qnnT, vsT, vwT, vaT, gnT) = _inproj(h, positions, w_in[i])
        k_c, v_cT = _compress(kc, vc, cmp_pos_emb[i], w_cmp_k1[i], w_cmp_k2[i], w_cmp_v1[i], w_cmp_v2[i])
        o_cmp, selb = _cmp_select(qnnT, k_c, v_cT, gnT)
        o_slc = _attention(qnrT, ks, vsT, tq=ATTN_TQ, rep=NSA_REP, selected=True, gnT=gnT, gate_branch=1,
                           selb=selb)
        o_win = _attention(qnrT, kw, vwT, tq=ATTN_TQ, rep=NSA_REP, nprev=NSA_WINDOW // ATTN_TQ, gnT=gnT,
                           gate_branch=2)
        o_swa = _attention(qaT, ka, vaT, tq=SWA_TQ, rep=SWA_HEADS, n_sub=SWA_SUBTILES,
                           nprev=SWA_WINDOW // SWA_TQ, sinks=attn_sinks[i].astype(F32))
        flat = lambda a: a.reshape(B * S, a.shape[-1])
        h1 = _outproj(flat(h), flat(o_swa), flat(o_cmp), flat(o_slc), flat(o_win), flat(gm),
                      w_proj_swa[i], w_proj_nsa[i], w_out[i], ln1_g[i], ln1_b[i], alpha)
        h2 = _ffn(h1, flat(p[i]), w_ff_gate[i], w_ff_up[i], w_ff_down[i], w_ple_gate[i], w_ple[i],
                  ln2_g[i], ln2_b[i], alpha)
        h = h2.reshape(B, S, D)
    return h
```

```python
import functools

import jax
import jax.numpy as jnp
from jax import lax
from jax.experimental import pallas as pl
from jax.experimental.pallas import tpu as pltpu

F32 = jnp.float32
BF16 = jnp.bfloat16

D_MODEL = 1024
HEAD_DIM = 64
HALF_DIM = HEAD_DIM // 2
ROPE_THETA = 10000.0
LN_EPS = 1e-5
NEG_INF = -1e30
FORCE = 1e30
TINY = 1e-30
PLE_DIM = 256

SWA_HEADS = 8
SWA_WINDOW = 128
NSA_HEADS = 8
NSA_GROUPS = 2
NSA_REP = NSA_HEADS // NSA_GROUPS
NSA_WINDOW = 512
CMP_BLOCK = 32
CMP_STRIDE = 16
CMP_HIDDEN = 256
SEL_BLOCK = 64
N_SEL = 16
D_FF = 2816

SWA_Q = SWA_HEADS * HEAD_DIM
NSA_Q = NSA_HEADS * HEAD_DIM
NSA_KV = NSA_GROUPS * HEAD_DIM
NSA_GATES = NSA_HEADS * 3
GATE_ROWS = 32
MAX_SEL_BLOCKS = 64
DEN_ROWS = 16
LOG2E = 1.4426950408889634

LANES = 128
VMEM_LIMIT = 56 * 1024 * 1024
MLP_VMEM_LIMIT = 60 * 1024 * 1024

TOK_K_COLS = 5 * LANES
TOK_COLS = TOK_K_COLS + 2 * D_MODEL
FEAT_V_ROWS = 2 * NSA_KV + HEAD_DIM + GATE_ROWS + 32
FEAT_ROWS = SWA_Q + NSA_Q + FEAT_V_ROWS

INPROJ_TM = 512
ATTN_TQ = 256
SEL_TQ = 256
SWA_TQ = 128
SWA_SUBTILES = 2
MLP_TM = 512
FF_CHUNKS = ((0, 1536), (1536, D_FF))
MASK_VAL = 2.0 * NEG_INF
SUBLANES = 8


def _dot(a, b):
    return jnp.dot(a, b, preferred_element_type=F32)


def _dot_nt(a, b):
    return lax.dot_general(a, b, (((1,), (1,)), ((), ())), preferred_element_type=F32)


def _inproj_kernel(x_ref, posc_ref, posr_ref, invr_ref, invc_ref, wn_ref, wt_ref,
                   ks_ref, kw_ref, ka_ref, kc_ref, vc_ref, gm_ref,
                   qaT_ref, qnrT_ref, qnnT_ref, vsT_ref, vwT_ref, vaT_ref, gnT_ref):
    tm = x_ref.shape[1]
    xb = x_ref[0].astype(BF16)

    ang = posc_ref[0] * invr_ref[...]
    cos_t = jnp.cos(ang)
    sin_t = jnp.sin(ang)
    lane = lax.broadcasted_iota(jnp.int32, (tm, LANES), 1)
    first = (lane & (HEAD_DIM - 1)) < HALF_DIM
    sin_s = jnp.where(first, -sin_t, sin_t)

    def rope_tok(z):
        partner = jnp.where(first, pltpu.roll(z, LANES - HALF_DIM, 1), pltpu.roll(z, HALF_DIM, 1))
        return z * cos_t + partner * sin_s

    zk = _dot(xb, wn_ref[:, 0:TOK_K_COLS])
    ks_ref[0] = rope_tok(zk[:, 0:128]).astype(BF16)
    kw_ref[0] = rope_tok(zk[:, 128:256]).astype(BF16)
    ka_ref[0] = rope_tok(zk[:, 256:384]).astype(BF16)
    kc_ref[0] = zk[:, 384:512]
    vc_ref[0] = zk[:, 512:640]
    gchunk = 512
    for c in range(2 * D_MODEL // gchunk):
        zg = _dot(xb, wn_ref[:, TOK_K_COLS + c * gchunk:TOK_K_COLS + (c + 1) * gchunk])
        gm_ref[0, :, c * gchunk:(c + 1) * gchunk] = jax.nn.sigmoid(zg).astype(BF16)

    ang_t = invc_ref[...] * posr_ref[0]
    cos_f = jnp.cos(ang_t)
    sin_f = jnp.sin(ang_t)
    scale = HEAD_DIM ** -0.5 * LOG2E

    def rope_feat_store(z, out_ref, n_heads):
        for h in range(n_heads):
            x1 = z[HEAD_DIM * h:HEAD_DIM * h + HALF_DIM]
            x2 = z[HEAD_DIM * h + HALF_DIM:HEAD_DIM * (h + 1)]
            out_ref[0, HEAD_DIM * h:HEAD_DIM * h + HALF_DIM, :] = ((x1 * cos_f - x2 * sin_f) * scale).astype(BF16)
            out_ref[0, HEAD_DIM * h + HALF_DIM:HEAD_DIM * (h + 1), :] = ((x2 * cos_f + x1 * sin_f) * scale).astype(BF16)

    zqa = _dot_nt(wt_ref[0:SWA_Q, :], xb)
    rope_feat_store(zqa, qaT_ref, SWA_HEADS)
    zqn = _dot_nt(wt_ref[SWA_Q:SWA_Q + NSA_Q, :], xb)
    rope_feat_store(zqn, qnrT_ref, NSA_HEADS)
    qnnT_ref[0] = (zqn * scale).astype(BF16)
    zv = _dot_nt(wt_ref[SWA_Q + NSA_Q:FEAT_ROWS, :], xb)
    for j in range(tm // SEL_TQ):
        sl = slice(j * SEL_TQ, (j + 1) * SEL_TQ)
        vsT_ref[0, j] = zv[0:NSA_KV, sl].astype(BF16)
    for j in range(tm // ATTN_TQ):
        sl = slice(j * ATTN_TQ, (j + 1) * ATTN_TQ)
        vwT_ref[0, j] = zv[NSA_KV:2 * NSA_KV, sl].astype(BF16)
    for j in range(tm // SWA_TQ):
        sl = slice(j * SWA_TQ, (j + 1) * SWA_TQ)
        vaT_ref[0, j] = zv[2 * NSA_KV:2 * NSA_KV + HEAD_DIM, sl].astype(BF16)
    g0 = 2 * NSA_KV + HEAD_DIM
    gnT_ref[0] = jax.nn.sigmoid(zv[g0:g0 + GATE_ROWS])


def _inproj(x, positions, w_in):
    B, S, D = x.shape
    tm = INPROJ_TM
    offs = {}
    off = 0
    for name, n in (("qa", SWA_Q), ("ka", HEAD_DIM), ("va", HEAD_DIM), ("qn", NSA_Q), ("kc", NSA_KV),
                    ("vc", NSA_KV), ("ks", NSA_KV), ("vs", NSA_KV), ("kw", NSA_KV), ("vw", NSA_KV),
                    ("gn", NSA_GATES), ("gm", 2 * D_MODEL)):
        offs[name] = (off, off + n)
        off += n
    col = lambda name: w_in[:, offs[name][0]:offs[name][1]]
    zpad = lambda n: jnp.zeros((D, n), w_in.dtype)
    wn = jnp.concatenate([col("ks"), col("kw"), col("ka"), zpad(LANES - HEAD_DIM), col("kc"), col("vc"),
                          col("gm")], axis=1).astype(BF16)
    wt = jnp.concatenate([col("qa"), col("qn"), col("vs"), col("vw"), col("va"), col("gn"),
                          zpad(FEAT_ROWS - (SWA_Q + NSA_Q + 2 * NSA_KV + HEAD_DIM + NSA_GATES))],
                         axis=1).T.astype(BF16)
    assert wn.shape == (D, TOK_COLS) and wt.shape == (FEAT_ROWS, D)

    inv = ROPE_THETA ** (-jnp.arange(0, HEAD_DIM, 2, dtype=F32) / HEAD_DIM)
    inv_row = jnp.tile(inv, LANES // HALF_DIM)[None, :]
    inv_col = inv[:, None]
    posf = positions.astype(F32)
    pos_col = posf[:, :, None]
    pos_row = posf[:, None, :]

    tok = lambda w: pl.BlockSpec((1, tm, w), lambda b, i: (b, i, 0))
    feat = lambda r: pl.BlockSpec((1, r, tm), lambda b, i: (b, 0, i))
    const = lambda shp: pl.BlockSpec(shp, lambda b, i: (0,) * len(shp))
    out_shape = (
        jax.ShapeDtypeStruct((B, S, LANES), BF16),
        jax.ShapeDtypeStruct((B, S, LANES), BF16),
        jax.ShapeDtypeStruct((B, S, LANES), BF16),
        jax.ShapeDtypeStruct((B, S, LANES), F32),
        jax.ShapeDtypeStruct((B, S, LANES), F32),
        jax.ShapeDtypeStruct((B, S, 2 * D_MODEL), BF16),
        jax.ShapeDtypeStruct((B, SWA_Q, S), BF16),
        jax.ShapeDtypeStruct((B, NSA_Q, S), BF16),
        jax.ShapeDtypeStruct((B, NSA_Q, S), BF16),
        jax.ShapeDtypeStruct((B, S // SEL_TQ, NSA_KV, SEL_TQ), BF16),
        jax.ShapeDtypeStruct((B, S // ATTN_TQ, NSA_KV, ATTN_TQ), BF16),
        jax.ShapeDtypeStruct((B, S // SWA_TQ, HEAD_DIM, SWA_TQ), BF16),
        jax.ShapeDtypeStruct((B, GATE_ROWS, S), F32),
    )
    out_specs = (
        tok(LANES), tok(LANES), tok(LANES), tok(LANES), tok(LANES), tok(2 * D_MODEL),
        feat(SWA_Q), feat(NSA_Q), feat(NSA_Q),
        pl.BlockSpec((1, tm // SEL_TQ, NSA_KV, SEL_TQ), lambda b, i: (b, i, 0, 0)),
        pl.BlockSpec((1, tm // ATTN_TQ, NSA_KV, ATTN_TQ), lambda b, i: (b, i, 0, 0)),
        pl.BlockSpec((1, tm // SWA_TQ, HEAD_DIM, SWA_TQ), lambda b, i: (b, i, 0, 0)),
        feat(GATE_ROWS),
    )
    return pl.pallas_call(
        _inproj_kernel,
        grid=(B, S // tm),
        in_specs=[tok(D), tok(1), pl.BlockSpec((1, 1, tm), lambda b, i: (b, 0, i)),
                  const((1, LANES)), const((HALF_DIM, 1)), const((D, TOK_COLS)), const((FEAT_ROWS, D))],
        out_specs=out_specs,
        out_shape=out_shape,
        compiler_params=pltpu.CompilerParams(dimension_semantics=("parallel", "parallel"),
                                             vmem_limit_bytes=VMEM_LIMIT),
        name="inproj",
    )(x, pos_col, pos_row, inv_row, inv_col, wn, wt)


def _gelu_tanh(x):
    return 0.5 * x * (1.0 + jnp.tanh(0.7978845608028654 * (x + 0.044715 * (x * x * x))))


def _compress_kernel(kc_ref, vc_ref, pe_ref, wbk_ref, w1k_ref, w2k_ref, wbv_ref, w1v_ref, w2vT_ref,
                     kcmp_ref, vcmpT_ref):
    n_ch = kc_ref.shape[1] // CMP_STRIDE

    def hidden(x_ref, wb_ref, w1_ref, pe_row):
        acc = jnp.zeros((n_ch, 2 * NSA_GROUPS * CMP_HIDDEN), F32)
        for t in range(CMP_STRIDE):
            x_t = x_ref[0, pl.ds(t, n_ch, stride=CMP_STRIDE), :].astype(BF16)
            acc = acc + _dot(x_t, wb_ref[t])
        bias = _dot(jnp.broadcast_to(pe_row, (SUBLANES, pe_row.shape[1])).astype(BF16), w1_ref[...])[0:1]
        out = []
        for g in range(NSA_GROUPS):
            u = acc[:, 2 * CMP_HIDDEN * g:2 * CMP_HIDDEN * g + CMP_HIDDEN]
            v = acc[:, 2 * CMP_HIDDEN * g + CMP_HIDDEN:2 * CMP_HIDDEN * (g + 1)]
            out.append(_gelu_tanh(u + pltpu.roll(v, n_ch - 1, 0) + bias).astype(BF16))
        return out

    hk = hidden(kc_ref, wbk_ref, w1k_ref, pe_ref[0:1, :])
    hv = hidden(vc_ref, wbv_ref, w1v_ref, pe_ref[1:2, :])
    for g in range(NSA_GROUPS):
        kcmp_ref[0, g] = _dot(hk[g], w2k_ref[...]).astype(BF16)
        vcmpT_ref[0, g] = _dot_nt(w2vT_ref[...], hv[g]).astype(BF16)


def _compress(kc, vc, pos_emb, w_k1, w_k2, w_v1, w_v2):
    B, S, _ = kc.shape
    n_ch = S // CMP_STRIDE
    half = CMP_STRIDE * HEAD_DIM

    def per_token(w1):
        blk = jnp.concatenate([w1[:half].reshape(CMP_STRIDE, HEAD_DIM, CMP_HIDDEN),
                               w1[half:].reshape(CMP_STRIDE, HEAD_DIM, CMP_HIDDEN)], axis=2)
        z = jnp.zeros_like(blk)
        return jnp.concatenate([jnp.concatenate([blk, z], axis=2), jnp.concatenate([z, blk], axis=2)],
                               axis=1).astype(BF16)

    pe = pos_emb.reshape(2, 2 * half)
    const = lambda shp: pl.BlockSpec(shp, lambda b: (0,) * len(shp))
    wb_shape = (CMP_STRIDE, NSA_KV, 2 * NSA_GROUPS * CMP_HIDDEN)
    return pl.pallas_call(
        _compress_kernel,
        grid=(B,),
        in_specs=[pl.BlockSpec((1, S, NSA_KV), lambda b: (b, 0, 0)),
                  pl.BlockSpec((1, S, NSA_KV), lambda b: (b, 0, 0)),
                  const((2, 2 * half)),
                  const(wb_shape), const((2 * half, CMP_HIDDEN)), const((CMP_HIDDEN, HEAD_DIM)),
                  const(wb_shape), const((2 * half, CMP_HIDDEN)), const((HEAD_DIM, CMP_HIDDEN))],
        out_specs=(pl.BlockSpec((1, NSA_GROUPS, n_ch, HEAD_DIM), lambda b: (b, 0, 0, 0)),
                   pl.BlockSpec((1, NSA_GROUPS, HEAD_DIM, n_ch), lambda b: (b, 0, 0, 0))),
        out_shape=(jax.ShapeDtypeStruct((B, NSA_GROUPS, n_ch, HEAD_DIM), BF16),
                   jax.ShapeDtypeStruct((B, NSA_GROUPS, HEAD_DIM, n_ch), BF16)),
        compiler_params=pltpu.CompilerParams(dimension_semantics=("parallel",),
                                             vmem_limit_bytes=VMEM_LIMIT),
        name="compress",
    )(kc, vc, pe, per_token(w_k1), w_k1.astype(BF16), w_k2.astype(BF16),
      per_token(w_v1), w_v1.astype(BF16), w_v2.T.astype(BF16))


def _store_head_pairs(o_ref, heads, rows=slice(None)):
    for pr in range(len(heads) // 2):
        pair = jnp.concatenate([heads[2 * pr], heads[2 * pr + 1]], axis=0)
        o_ref[0, rows, LANES * pr:LANES * (pr + 1)] = pair.T.astype(BF16)


def _cmp_select_kernel(qT_ref, kc_ref, vcT_ref, gT_ref, o_ref, selb_ref, s_ref, psum_ref, imp_ref, rank_ref, *,
                       n_blk, n_sel):
    tq = qT_ref.shape[2]
    n_ch = kc_ref.shape[2]
    t = pl.program_id(1) * tq + lax.broadcasted_iota(jnp.int32, (1, tq), 1)
    cidx = lax.broadcasted_iota(jnp.int32, (n_ch, 1), 0)
    cmask = (cidx * CMP_STRIDE + (CMP_BLOCK - 1)) <= t
    blk = lax.broadcasted_iota(jnp.int32, (n_blk, tq), 0)
    cur = t >> 6
    forced = (blk == 0) | (blk == cur) | (blk == cur - 1)
    ratio = SEL_BLOCK // CMP_STRIDE

    for h in range(NSA_HEADS):
        s_ref[h] = _dot(kc_ref[0, h // NSA_REP], qT_ref[0, HEAD_DIM * h:HEAD_DIM * (h + 1), :])
    heads = []
    for g in range(NSA_GROUPS):
        psum = jnp.zeros((n_ch, tq), F32)
        for r in range(NSA_REP):
            h = NSA_REP * g + r
            s = jnp.where(cmask, s_ref[h], MASK_VAL)
            m = jnp.maximum(jnp.max(s, axis=0, keepdims=True), NEG_INF)
            e = jnp.exp2(s - m)
            den = jnp.sum(e, axis=0, keepdims=True)
            p = e * (1.0 / jnp.maximum(den, TINY))
            o = _dot(vcT_ref[0, g], p.astype(BF16))
            heads.append(o * gT_ref[0, 3 * h:3 * h + 1, :])
            psum = psum + p
        for j in range(tq // LANES):
            psum_ref[j] = psum[:, LANES * j:LANES * (j + 1)]
        rows = [jnp.concatenate([psum_ref[j, pl.ds(k, n_blk, stride=ratio), :] for j in range(tq // LANES)],
                                axis=1) for k in range(ratio)]
        prev = jnp.where(blk == 0, 0.0, pltpu.roll(rows[ratio - 1], 1, 0))
        imp = prev + rows[0] + rows[1] + rows[2] + rows[3]
        imp_ref[g, 0:n_blk, :] = jnp.where(forced, FORCE, jnp.where(blk <= cur, imp, NEG_INF))
        if n_blk < MAX_SEL_BLOCKS:
            imp_ref[g, n_blk:MAX_SEL_BLOCKS, :] = jnp.full((MAX_SEL_BLOCKS - n_blk, tq), NEG_INF, F32)
        rank_ref[g] = jnp.zeros((MAX_SEL_BLOCKS, tq), jnp.int32)
    _store_head_pairs(o_ref, heads)

    last_cur = (pl.program_id(1) * tq + tq - 1) >> 6
    sub = lax.broadcasted_iota(jnp.int32, (SUBLANES, tq), 0)
    n_groups = -(-n_blk // SUBLANES)
    for rg in range(n_groups):
        @pl.when(rg * SUBLANES <= last_cur)
        def _(rg=rg):
            for g in range(NSA_GROUPS):
                src = imp_ref[g, SUBLANES * rg:SUBLANES * (rg + 1), :]
                for v in range(n_groups):
                    tgt = imp_ref[g, SUBLANES * v:SUBLANES * (v + 1), :]
                    cnt = jnp.zeros((SUBLANES, tq), jnp.int32)
                    for r in range(SUBLANES):
                        row = src[r:r + 1, :]
                        if v > rg:
                            cnt = cnt + jnp.where(row >= tgt, 1, 0)
                        elif v < rg:
                            cnt = cnt + jnp.where(row > tgt, 1, 0)
                        else:
                            cnt = cnt + jnp.where(sub > r, jnp.where(row >= tgt, 1, 0), jnp.where(row > tgt, 1, 0))
                    rank_ref[g, SUBLANES * v:SUBLANES * (v + 1), :] += cnt
    for g in range(NSA_GROUPS):
        selb_ref[0, g] = jnp.where(rank_ref[g] < n_sel, 0.0, MASK_VAL).astype(BF16)


def _cmp_select(qnnT, k_c, v_cT, gnT):
    B, _, S = qnnT.shape
    tq = ATTN_TQ
    n_ch = k_c.shape[2]
    n_blk = S // SEL_BLOCK
    assert n_blk <= MAX_SEL_BLOCKS
    kern = functools.partial(_cmp_select_kernel, n_blk=n_blk, n_sel=min(N_SEL, n_blk))
    return pl.pallas_call(
        kern,
        grid=(B, S // tq),
        in_specs=[pl.BlockSpec((1, NSA_Q, tq), lambda b, i: (b, 0, i)),
                  pl.BlockSpec((1, NSA_GROUPS, n_ch, HEAD_DIM), lambda b, i: (b, 0, 0, 0)),
                  pl.BlockSpec((1, NSA_GROUPS, HEAD_DIM, n_ch), lambda b, i: (b, 0, 0, 0)),
                  pl.BlockSpec((1, GATE_ROWS, tq), lambda b, i: (b, 0, i))],
        out_specs=(pl.BlockSpec((1, tq, NSA_Q), lambda b, i: (b, i, 0)),
                   pl.BlockSpec((1, NSA_GROUPS, MAX_SEL_BLOCKS, tq), lambda b, i: (b, 0, 0, i))),
        out_shape=(jax.ShapeDtypeStruct((B, S, NSA_Q), BF16),
                   jax.ShapeDtypeStruct((B, NSA_GROUPS, MAX_SEL_BLOCKS, S), BF16)),
        scratch_shapes=[pltpu.VMEM((NSA_HEADS, n_ch, tq), F32),
                        pltpu.VMEM((tq // LANES, n_ch, LANES), F32),
                        pltpu.VMEM((NSA_GROUPS, MAX_SEL_BLOCKS, tq), F32),
                        pltpu.VMEM((NSA_GROUPS, MAX_SEL_BLOCKS, tq), jnp.int32)],
        compiler_params=pltpu.CompilerParams(dimension_semantics=("parallel", "parallel"),
                                             vmem_limit_bytes=VMEM_LIMIT),
        name="cmp_select",
    )(qnnT, k_c, v_cT, gnT)


def _attn_kernel(*refs, n_heads, rep, n_sub, nprev, selected, use_sink, gate_branch, interleave):
    refs = list(refs)
    qT_ref, k_ref, vT_ref = refs[:3]
    pos = 3
    gT_ref = selb_ref = e_ref = sink_ref = None
    if gate_branch is not None:
        gT_ref = refs[pos]; pos += 1
    if selected:
        selb_ref, e_ref = refs[pos], refs[pos + 1]; pos += 2
    if use_sink:
        sink_ref = refs[pos]; pos += 1
    o_ref = refs[pos]
    rhs_ref, sa_ref, sb_ref, acc_ref, m_ref = refs[pos + 1:pos + 6]

    tq = sa_ref.shape[1]
    ksub = lax.broadcasted_iota(jnp.int32, (tq, tq), 0)
    qlane = lax.broadcasted_iota(jnp.int32, (tq, tq), 1)
    zeros_h = jnp.zeros((HEAD_DIM, tq), BF16)
    ones_rows = jnp.ones((DEN_ROWS, tq), BF16)
    acc_rows = HEAD_DIM + DEN_ROWS
    den_row = lax.broadcasted_iota(jnp.int32, (acc_rows, tq), 0) >= HEAD_DIM
    groups = sorted({h // rep for h in range(n_heads)})

    def qk_head(c, h, dst_ref):
        start = pl.multiple_of(c * tq, tq)
        k = k_ref[0, pl.ds(start, tq), :]
        if selected:
            k = jnp.concatenate([k, e_ref[pl.ds(start, tq), :]], axis=1)
        dst_ref[h] = _dot(k, rhs_ref[h])

    def softmax_pv_head(c, h, src_ref, mask, bias):
        g = h // rep
        v_aug = jnp.concatenate([vT_ref[0, c, HEAD_DIM * g:HEAD_DIM * (g + 1), :], ones_rows], axis=0)
        s = src_ref[h]
        if bias is not None:
            s = s + bias
        if mask is not None:
            s = jnp.where(mask, s, MASK_VAL)
        m_old = m_ref[h]
        m_new = jnp.maximum(m_old, jnp.max(s, axis=0, keepdims=True))
        p = jnp.exp2((s - m_new).astype(BF16))
        acc_ref[h] = jnp.exp2(m_old - m_new) * acc_ref[h] + _dot(v_aug, p)
        m_ref[h] = m_new

    def qk(c, dst_ref):
        for h in range(n_heads):
            qk_head(c, h, dst_ref)

    def softmax_pv(c, src_ref, mask=None, bias=None, next_c=None, next_ref=None):
        if next_ref is not None and not interleave:
            qk(next_c, next_ref)
        for h in range(n_heads):
            if next_ref is not None and interleave:
                qk_head(next_c, h, next_ref)
            softmax_pv_head(c, h, src_ref, mask, bias)

    for sub in range(n_sub):
        i = pl.program_id(1) * n_sub + sub
        cols = slice(sub * tq, (sub + 1) * tq)
        for h in range(n_heads):
            g = h // rep
            parts = [zeros_h, zeros_h]
            parts[g] = qT_ref[0, HEAD_DIM * h:HEAD_DIM * (h + 1), cols]
            if selected:
                parts += [selb_ref[0, g, :, cols], zeros_h]
            rhs_ref[h] = jnp.concatenate(parts, axis=0)
            if use_sink:
                m_ref[h] = jnp.full((1, tq), sink_ref[h] * LOG2E, F32)
                acc_ref[h] = jnp.where(den_row, 1.0, 0.0)
            else:
                m_ref[h] = jnp.full((1, tq), NEG_INF, F32)
                acc_ref[h] = jnp.zeros((acc_rows, tq), F32)

        if selected:
            qk(0, sa_ref)

            def pair(j, carry):
                c = 2 * j
                softmax_pv(c, sa_ref, next_c=c + 1, next_ref=sb_ref)
                softmax_pv(c + 1, sb_ref, next_c=c + 2, next_ref=sa_ref)
                return carry

            lax.fori_loop(0, i // 2, pair, 0)
            odd = i % 2
            last = i - odd

            softmax_pv(last, sa_ref, mask=ksub <= qlane + odd * tq, next_c=i, next_ref=sb_ref)

            @pl.when(odd == 1)
            def _():
                softmax_pv(i, sb_ref, mask=ksub <= qlane)
        else:
            bufs = (sa_ref, sb_ref)
            chunks = [jnp.maximum(i - d, 0) for d in range(nprev, 0, -1)] + [i]
            qk(chunks[0], bufs[0])
            for n, d in enumerate(range(nprev, -1, -1)):
                nxt = dict(next_c=chunks[n + 1], next_ref=bufs[(n + 1) % 2]) if n + 1 < len(chunks) else {}
                if d == 0:
                    softmax_pv(chunks[n], bufs[n % 2], mask=ksub <= qlane, **nxt)
                elif d == nprev:
                    gone = jnp.where(i >= d, 0, tq)
                    softmax_pv(chunks[n], bufs[n % 2], mask=ksub > qlane + gone, **nxt)
                else:
                    softmax_pv(chunks[n], bufs[n % 2], bias=jnp.where(i >= d, 0.0, MASK_VAL), **nxt)

        heads = []
        for h in range(n_heads):
            acc = acc_ref[h]
            o = acc[0:HEAD_DIM] * (1.0 / jnp.maximum(acc[HEAD_DIM:HEAD_DIM + 1], TINY))
            if gate_branch is not None:
                o = o * gT_ref[0, 3 * h + gate_branch:3 * h + gate_branch + 1, cols]
            heads.append(o)
        _store_head_pairs(o_ref, heads, cols)


def _attention(qT, k, vT, *, tq, rep, n_sub=1, nprev=0, selected=False, gnT=None, gate_branch=None,
               selb=None, sinks=None, interleave=True):
    B, F, S = qT.shape
    n_heads = F // HEAD_DIM
    n_kv = vT.shape[2]
    tstep = tq * n_sub
    assert vT.shape == (B, S // tq, n_kv, tq) and S % tstep == 0
    in_specs = [pl.BlockSpec((1, F, tstep), lambda b, i: (b, 0, i)),
                pl.BlockSpec((1, S, LANES), lambda b, i: (b, 0, 0)),
                pl.BlockSpec((1, S // tq, n_kv, tq), lambda b, i: (b, 0, 0, 0))]
    args = [qT, k, vT]
    if gate_branch is not None:
        in_specs.append(pl.BlockSpec((1, GATE_ROWS, tstep), lambda b, i: (b, 0, i)))
        args.append(gnT)
    if selected:
        onehot = (jnp.arange(S)[:, None] // SEL_BLOCK == jnp.arange(LANES)[None, :]).astype(BF16)
        in_specs += [pl.BlockSpec((1, NSA_GROUPS, MAX_SEL_BLOCKS, tstep), lambda b, i: (b, 0, 0, i)),
                     pl.BlockSpec((S, LANES), lambda b, i: (0, 0))]
        args += [selb, onehot]
    if sinks is not None:
        in_specs.append(pl.BlockSpec(memory_space=pltpu.SMEM))
        args.append(sinks)
    kern = functools.partial(_attn_kernel, n_heads=n_heads, rep=rep, n_sub=n_sub, nprev=nprev,
                             selected=selected, use_sink=sinks is not None, gate_branch=gate_branch,
                             interleave=interleave)
    contraction = 2 * LANES if selected else LANES
    return pl.pallas_call(
        kern,
        grid=(B, S // tstep),
        in_specs=in_specs,
        out_specs=pl.BlockSpec((1, tstep, F), lambda b, i: (b, i, 0)),
        out_shape=jax.ShapeDtypeStruct((B, S, F), BF16),
        scratch_shapes=[pltpu.VMEM((n_heads, contraction, tq), BF16),
                        pltpu.VMEM((n_heads, tq, tq), F32),
                        pltpu.VMEM((n_heads, tq, tq), F32),
                        pltpu.VMEM((n_heads, HEAD_DIM + DEN_ROWS, tq), F32),
                        pltpu.VMEM((n_heads, 1, tq), F32)],
        compiler_params=pltpu.CompilerParams(dimension_semantics=("parallel", "parallel"),
                                             vmem_limit_bytes=VMEM_LIMIT),
        name="attn_sel" if selected else ("attn_swa" if sinks is not None else "attn_win"),
    )(*args)


def _layer_norm(r, g, b):
    mu = jnp.mean(r, axis=-1, keepdims=True)
    d = r - mu
    var = jnp.mean(d * d, axis=-1, keepdims=True)
    return d * lax.rsqrt(var + LN_EPS) * g + b


def _mlp_kernel(x_ref, oa_ref, oc_ref, os_ref, ow_ref, gm_ref, p_ref, wa_ref, wb_ref, wo_ref, g1_ref, b1_ref,
                wg_ref, wu_ref, wd_ref, wpg_ref, wple_ref, g2_ref, b2_ref, o_ref, *, alpha):
    o_b = (oc_ref[...].astype(F32) + os_ref[...].astype(F32) + ow_ref[...].astype(F32)).astype(BF16)
    y = (gm_ref[:, 0:D_MODEL].astype(F32) * _dot(oa_ref[...], wa_ref[...])
         + gm_ref[:, D_MODEL:2 * D_MODEL].astype(F32) * _dot(o_b, wb_ref[...]))
    mix = _dot(y.astype(BF16), wo_ref[...])
    h = _layer_norm(alpha * x_ref[...] + mix, g1_ref[...], b1_ref[...])
    hb = h.astype(BF16)
    ff = jnp.zeros(h.shape, F32)
    for lo, hi in FF_CHUNKS:
        sl = slice(lo, hi)
        a = jax.nn.silu(_dot(hb, wg_ref[:, sl])) * _dot(hb, wu_ref[:, sl])
        ff = ff + _dot(a.astype(BF16), wd_ref[sl, :])
    ple = jax.nn.sigmoid(_dot(hb, wpg_ref[...])) * _dot(p_ref[...].astype(BF16), wple_ref[...])
    o_ref[...] = _layer_norm(alpha * h + ff + ple, g2_ref[...], b2_ref[...])


def _mlp(x2, o_a, o_c, o_s, o_w, gm, p2, w_a, w_b, w_o, ln1_g, ln1_b, w_g, w_u, w_d, w_pg, w_ple, ln2_g, ln2_b,
         alpha):
    T, D = x2.shape
    tm = MLP_TM
    row = lambda w: pl.BlockSpec((tm, w), lambda i: (i, 0))
    const = lambda shp: pl.BlockSpec(shp, lambda i: (0, 0), pipeline_mode=pl.Buffered(1))
    bf = lambda w: w.astype(BF16)
    return pl.pallas_call(
        functools.partial(_mlp_kernel, alpha=alpha),
        grid=(T // tm,),
        in_specs=[row(D), row(SWA_Q), row(NSA_Q), row(NSA_Q), row(NSA_Q), row(2 * D), row(PLE_DIM),
                  const((SWA_Q, D)), const((NSA_Q, D)), const((D, D)), const((1, D)), const((1, D)),
                  const((D, D_FF)), const((D, D_FF)), const((D_FF, D)), const((D, D)), const((PLE_DIM, D)),
                  const((1, D)), const((1, D))],
        out_specs=row(D),
        out_shape=jax.ShapeDtypeStruct((T, D), F32),
        compiler_params=pltpu.CompilerParams(dimension_semantics=("parallel",),
                                             vmem_limit_bytes=MLP_VMEM_LIMIT),
        name="mlp_ln",
    )(x2, o_a, o_c, o_s, o_w, gm, p2, bf(w_a), bf(w_b), bf(w_o), ln1_g[None, :], ln1_b[None, :],
      bf(w_g), bf(w_u), bf(w_d), bf(w_pg), bf(w_ple), ln2_g[None, :], ln2_b[None, :])


def kernel(x, p, positions, w_in, attn_sinks, cmp_pos_emb, w_cmp_k1, w_cmp_k2, w_cmp_v1, w_cmp_v2, w_proj_swa, w_proj_nsa, w_out, ln1_g, ln1_b, w_ff_gate, w_ff_up, w_ff_down, w_ple, w_ple_gate, ln2_g, ln2_b):
    B, S, D = x.shape
    depth = w_in.shape[0]
    assert D == D_MODEL and S % INPROJ_TM == 0 and S >= NSA_WINDOW + ATTN_TQ
    alpha = (2.0 * depth) ** 0.25
    h = x
    for i in range(depth):
        (ks, kw, ka, kc, vc, gm, qaT, qnrT, qnnT, vsT, vwT, vaT, gnT) = _inproj(h, positions, w_in[i])
        k_c, v_cT = _compress(kc, vc, cmp_pos_emb[i], w_cmp_k1[i], w_cmp_k2[i], w_cmp_v1[i], w_cmp_v2[i])
        o_cmp, selb = _cmp_select(qnnT, k_c, v_cT, gnT)
        o_slc = _attention(qnrT, ks, vsT, tq=SEL_TQ, rep=NSA_REP, selected=True, gnT=gnT, gate_branch=1,
                           selb=selb)
        o_win = _attention(qnrT, kw, vwT, tq=ATTN_TQ, rep=NSA_REP, nprev=NSA_WINDOW // ATTN_TQ, gnT=gnT,
                           gate_branch=2)
        o_swa = _attention(qaT, ka, vaT, tq=SWA_TQ, rep=SWA_HEADS, n_sub=SWA_SUBTILES,
                           nprev=SWA_WINDOW // SWA_TQ, sinks=attn_sinks[i].astype(F32), interleave=False)
        flat = lambda a: a.reshape(B * S, a.shape[-1])
        h2 = _mlp(flat(h), flat(o_swa), flat(o_cmp), flat(o_slc), flat(o_win), flat(gm), flat(p[i]),
                  w_proj_swa[i], w_proj_nsa[i], w_out[i], ln1_g[i], ln1_b[i],
                  w_ff_gate[i], w_ff_up[i], w_ff_down[i], w_ple_gate[i], w_ple[i], ln2_g[i], ln2_b[i], alpha)
        h = h2.reshape(B, S, D)
    return h
```

```python
import functools

import jax
import jax.numpy as jnp
from jax import lax
from jax.experimental import pallas as pl
from jax.experimental.pallas import tpu as pltpu

F32 = jnp.float32
BF16 = jnp.bfloat16

D_MODEL = 1024
HEAD_DIM = 64
HALF_DIM = HEAD_DIM // 2
ROPE_THETA = 10000.0
LN_EPS = 1e-5
NEG_INF = -1e30
FORCE = 1e30
TINY = 1e-30
PLE_DIM = 256

SWA_HEADS = 8
SWA_WINDOW = 128
NSA_HEADS = 8
NSA_GROUPS = 2
NSA_REP = NSA_HEADS // NSA_GROUPS
NSA_WINDOW = 512
CMP_BLOCK = 32
CMP_STRIDE = 16
CMP_HIDDEN = 256
SEL_BLOCK = 64
N_SEL = 16
D_FF = 2816

SWA_Q = SWA_HEADS * HEAD_DIM
NSA_Q = NSA_HEADS * HEAD_DIM
NSA_KV = NSA_GROUPS * HEAD_DIM
NSA_GATES = NSA_HEADS * 3
GATE_ROWS = 32
MAX_SEL_BLOCKS = 64
DEN_ROWS = 16
LOG2E = 1.4426950408889634

LANES = 128
VMEM_LIMIT = 56 * 1024 * 1024
MLP_VMEM_LIMIT = 60 * 1024 * 1024

TOK_KV_COLS = 2 * LANES
TOK_COLS = TOK_KV_COLS + 2 * D_MODEL
FEAT_K_ROWS = 3 * LANES
FEAT_V_ROWS = 2 * NSA_KV + HEAD_DIM + GATE_ROWS + 32
FEAT_ROWS = SWA_Q + NSA_Q + FEAT_K_ROWS + FEAT_V_ROWS

INPROJ_TM = 512
ATTN_TQ = 256
SEL_TQ = 256
SWA_TQ = 128
SWA_SUBTILES = 2
MLP_TM = 512
MLP_SPLIT = 2
FF_CHUNKS = ((0, 1536), (1536, D_FF))
MASK_VAL = 2.0 * NEG_INF
SUBLANES = 8


def _dot(a, b):
    return jnp.dot(a, b, preferred_element_type=F32)


def _dot_nt(a, b):
    return lax.dot_general(a, b, (((1,), (1,)), ((), ())), preferred_element_type=F32)


def _inproj_kernel(x_ref, posr_ref, invc_ref, wn_ref, wt_ref,
                   ks_ref, kw_ref, ka_ref, kc_ref, vc_ref, gm_ref,
                   qaT_ref, qnrT_ref, qnnT_ref, vsT_ref, vwT_ref, vaT_ref, gnT_ref):
    tm = x_ref.shape[1]
    xb = x_ref[0].astype(BF16)

    zkv = _dot(xb, wn_ref[:, 0:TOK_KV_COLS])
    kc_ref[0] = zkv[:, 0:LANES]
    vc_ref[0] = zkv[:, LANES:2 * LANES]
    gchunk = 512
    for c in range(2 * D_MODEL // gchunk):
        zg = _dot(xb, wn_ref[:, TOK_KV_COLS + c * gchunk:TOK_KV_COLS + (c + 1) * gchunk])
        gm_ref[0, :, c * gchunk:(c + 1) * gchunk] = jax.nn.sigmoid(zg).astype(BF16)

    ang_t = invc_ref[...] * posr_ref[0]
    cos_f = jnp.cos(ang_t)
    sin_f = jnp.sin(ang_t)
    scale = HEAD_DIM ** -0.5 * LOG2E

    def rope_feat(z, n_heads, mul):
        out = []
        for h in range(n_heads):
            x1 = z[HEAD_DIM * h:HEAD_DIM * h + HALF_DIM]
            x2 = z[HEAD_DIM * h + HALF_DIM:HEAD_DIM * (h + 1)]
            out += [(x1 * cos_f - x2 * sin_f) * mul, (x2 * cos_f + x1 * sin_f) * mul]
        return out

    def store_rows(out_ref, pieces):
        for n, piece in enumerate(pieces):
            out_ref[0, HALF_DIM * n:HALF_DIM * (n + 1), :] = piece.astype(BF16)

    zqa = _dot_nt(wt_ref[0:SWA_Q, :], xb)
    store_rows(qaT_ref, rope_feat(zqa, SWA_HEADS, scale))
    zqn = _dot_nt(wt_ref[SWA_Q:SWA_Q + NSA_Q, :], xb)
    store_rows(qnrT_ref, rope_feat(zqn, NSA_HEADS, scale))
    qnnT_ref[0] = (zqn * scale).astype(BF16)
    k0 = SWA_Q + NSA_Q
    zk = _dot_nt(wt_ref[k0:k0 + FEAT_K_ROWS, :], xb)
    kr = rope_feat(zk, FEAT_K_ROWS // HEAD_DIM, 1.0)
    for j, out_ref in enumerate((ks_ref, kw_ref, ka_ref)):
        blk = jnp.concatenate(kr[4 * j:4 * (j + 1)], axis=0)
        out_ref[0] = blk.T.astype(BF16)
    zv = _dot_nt(wt_ref[k0 + FEAT_K_ROWS:FEAT_ROWS, :], xb)
    for j in range(tm // SEL_TQ):
        sl = slice(j * SEL_TQ, (j + 1) * SEL_TQ)
        vsT_ref[0, j] = zv[0:NSA_KV, sl].astype(BF16)
    for j in range(tm // ATTN_TQ):
        sl = slice(j * ATTN_TQ, (j + 1) * ATTN_TQ)
        vwT_ref[0, j] = zv[NSA_KV:2 * NSA_KV, sl].astype(BF16)
    for j in range(tm // SWA_TQ):
        sl = slice(j * SWA_TQ, (j + 1) * SWA_TQ)
        vaT_ref[0, j] = zv[2 * NSA_KV:2 * NSA_KV + HEAD_DIM, sl].astype(BF16)
    g0 = 2 * NSA_KV + HEAD_DIM
    gnT_ref[0] = jax.nn.sigmoid(zv[g0:g0 + GATE_ROWS])


def _inproj(x, positions, w_in):
    B, S, D = x.shape
    tm = INPROJ_TM
    offs = {}
    off = 0
    for name, n in (("qa", SWA_Q), ("ka", HEAD_DIM), ("va", HEAD_DIM), ("qn", NSA_Q), ("kc", NSA_KV),
                    ("vc", NSA_KV), ("ks", NSA_KV), ("vs", NSA_KV), ("kw", NSA_KV), ("vw", NSA_KV),
                    ("gn", NSA_GATES), ("gm", 2 * D_MODEL)):
        offs[name] = (off, off + n)
        off += n
    col = lambda name: w_in[:, offs[name][0]:offs[name][1]]
    zpad = lambda n: jnp.zeros((D, n), w_in.dtype)
    wn = jnp.concatenate([col("kc"), col("vc"), col("gm")], axis=1).astype(BF16)
    wt = jnp.concatenate([col("qa"), col("qn"), col("ks"), col("kw"), col("ka"), zpad(LANES - HEAD_DIM),
                          col("vs"), col("vw"), col("va"), col("gn"),
                          zpad(FEAT_V_ROWS - (2 * NSA_KV + HEAD_DIM + NSA_GATES))], axis=1).T.astype(BF16)
    assert wn.shape == (D, TOK_COLS) and wt.shape == (FEAT_ROWS, D)

    inv_col = (ROPE_THETA ** (-jnp.arange(0, HEAD_DIM, 2, dtype=F32) / HEAD_DIM))[:, None]
    pos_row = positions.astype(F32)[:, None, :]

    tok = lambda w: pl.BlockSpec((1, tm, w), lambda b, i: (b, i, 0))
    feat = lambda r: pl.BlockSpec((1, r, tm), lambda b, i: (b, 0, i))
    const = lambda shp: pl.BlockSpec(shp, lambda b, i: (0,) * len(shp))
    out_shape = (
        jax.ShapeDtypeStruct((B, S, LANES), BF16),
        jax.ShapeDtypeStruct((B, S, LANES), BF16),
        jax.ShapeDtypeStruct((B, S, LANES), BF16),
        jax.ShapeDtypeStruct((B, S, LANES), F32),
        jax.ShapeDtypeStruct((B, S, LANES), F32),
        jax.ShapeDtypeStruct((B, S, 2 * D_MODEL), BF16),
        jax.ShapeDtypeStruct((B, SWA_Q, S), BF16),
        jax.ShapeDtypeStruct((B, NSA_Q, S), BF16),
        jax.ShapeDtypeStruct((B, NSA_Q, S), BF16),
        jax.ShapeDtypeStruct((B, S // SEL_TQ, NSA_KV, SEL_TQ), BF16),
        jax.ShapeDtypeStruct((B, S // ATTN_TQ, NSA_KV, ATTN_TQ), BF16),
        jax.ShapeDtypeStruct((B, S // SWA_TQ, HEAD_DIM, SWA_TQ), BF16),
        jax.ShapeDtypeStruct((B, GATE_ROWS, S), F32),
    )
    out_specs = (
        tok(LANES), tok(LANES), tok(LANES), tok(LANES), tok(LANES), tok(2 * D_MODEL),
        feat(SWA_Q), feat(NSA_Q), feat(NSA_Q),
        pl.BlockSpec((1, tm // SEL_TQ, NSA_KV, SEL_TQ), lambda b, i: (b, i, 0, 0)),
        pl.BlockSpec((1, tm // ATTN_TQ, NSA_KV, ATTN_TQ), lambda b, i: (b, i, 0, 0)),
        pl.BlockSpec((1, tm // SWA_TQ, HEAD_DIM, SWA_TQ), lambda b, i: (b, i, 0, 0)),
        feat(GATE_ROWS),
    )
    return pl.pallas_call(
        _inproj_kernel,
        grid=(B, S // tm),
        in_specs=[tok(D), pl.BlockSpec((1, 1, tm), lambda b, i: (b, 0, i)),
                  const((HALF_DIM, 1)), const((D, TOK_COLS)), const((FEAT_ROWS, D))],
        out_specs=out_specs,
        out_shape=out_shape,
        compiler_params=pltpu.CompilerParams(dimension_semantics=("parallel", "parallel"),
                                             vmem_limit_bytes=VMEM_LIMIT),
        name="inproj",
    )(x, pos_row, inv_col, wn, wt)


def _gelu_tanh(x):
    return 0.5 * x * (1.0 + jnp.tanh(0.7978845608028654 * (x + 0.044715 * (x * x * x))))


def _compress_kernel(kc_ref, vc_ref, pe_ref, wbk_ref, w1k_ref, w2k_ref, wbv_ref, w1v_ref, w2vT_ref,
                     kcmp_ref, vcmpT_ref):
    n_ch = kc_ref.shape[1] // CMP_STRIDE

    def hidden(x_ref, wb_ref, w1_ref, pe_row):
        acc = jnp.zeros((n_ch, 2 * NSA_GROUPS * CMP_HIDDEN), F32)
        for t in range(CMP_STRIDE):
            x_t = x_ref[0, pl.ds(t, n_ch, stride=CMP_STRIDE), :].astype(BF16)
            acc = acc + _dot(x_t, wb_ref[t])
        bias = _dot(jnp.broadcast_to(pe_row, (SUBLANES, pe_row.shape[1])).astype(BF16), w1_ref[...])[0:1]
        out = []
        for g in range(NSA_GROUPS):
            u = acc[:, 2 * CMP_HIDDEN * g:2 * CMP_HIDDEN * g + CMP_HIDDEN]
            v = acc[:, 2 * CMP_HIDDEN * g + CMP_HIDDEN:2 * CMP_HIDDEN * (g + 1)]
            out.append(_gelu_tanh(u + pltpu.roll(v, n_ch - 1, 0) + bias).astype(BF16))
        return out

    hk = hidden(kc_ref, wbk_ref, w1k_ref, pe_ref[0:1, :])
    hv = hidden(vc_ref, wbv_ref, w1v_ref, pe_ref[1:2, :])
    for g in range(NSA_GROUPS):
        kcmp_ref[0, g] = _dot(hk[g], w2k_ref[...]).astype(BF16)
        vcmpT_ref[0, g] = _dot_nt(w2vT_ref[...], hv[g]).astype(BF16)


def _compress(kc, vc, pos_emb, w_k1, w_k2, w_v1, w_v2):
    B, S, _ = kc.shape
    n_ch = S // CMP_STRIDE
    half = CMP_STRIDE * HEAD_DIM

    def per_token(w1):
        blk = jnp.concatenate([w1[:half].reshape(CMP_STRIDE, HEAD_DIM, CMP_HIDDEN),
                               w1[half:].reshape(CMP_STRIDE, HEAD_DIM, CMP_HIDDEN)], axis=2)
        z = jnp.zeros_like(blk)
        return jnp.concatenate([jnp.concatenate([blk, z], axis=2), jnp.concatenate([z, blk], axis=2)],
                               axis=1).astype(BF16)

    pe = pos_emb.reshape(2, 2 * half)
    const = lambda shp: pl.BlockSpec(shp, lambda b: (0,) * len(shp))
    wb_shape = (CMP_STRIDE, NSA_KV, 2 * NSA_GROUPS * CMP_HIDDEN)
    return pl.pallas_call(
        _compress_kernel,
        grid=(B,),
        in_specs=[pl.BlockSpec((1, S, NSA_KV), lambda b: (b, 0, 0)),
                  pl.BlockSpec((1, S, NSA_KV), lambda b: (b, 0, 0)),
                  const((2, 2 * half)),
                  const(wb_shape), const((2 * half, CMP_HIDDEN)), const((CMP_HIDDEN, HEAD_DIM)),
                  const(wb_shape), const((2 * half, CMP_HIDDEN)), const((HEAD_DIM, CMP_HIDDEN))],
        out_specs=(pl.BlockSpec((1, NSA_GROUPS, n_ch, HEAD_DIM), lambda b: (b, 0, 0, 0)),
                   pl.BlockSpec((1, NSA_GROUPS, HEAD_DIM, n_ch), lambda b: (b, 0, 0, 0))),
        out_shape=(jax.ShapeDtypeStruct((B, NSA_GROUPS, n_ch, HEAD_DIM), BF16),
                   jax.ShapeDtypeStruct((B, NSA_GROUPS, HEAD_DIM, n_ch), BF16)),
        compiler_params=pltpu.CompilerParams(dimension_semantics=("parallel",),
                                             vmem_limit_bytes=VMEM_LIMIT),
        name="compress",
    )(kc, vc, pe, per_token(w_k1), w_k1.astype(BF16), w_k2.astype(BF16),
      per_token(w_v1), w_v1.astype(BF16), w_v2.T.astype(BF16))


def _store_head_pairs(o_ref, heads, rows=slice(None)):
    for pr in range(len(heads) // 2):
        pair = jnp.concatenate([heads[2 * pr], heads[2 * pr + 1]], axis=0)
        o_ref[0, rows, LANES * pr:LANES * (pr + 1)] = pair.T.astype(BF16)


def _cmp_select_kernel(qT_ref, kc_ref, vcT_ref, gT_ref, o_ref, selb_ref, s_ref, psum_ref, imp_ref, rank_ref, *,
                       n_blk, n_sel):
    tq = qT_ref.shape[2]
    n_ch = kc_ref.shape[2]
    t = pl.program_id(1) * tq + lax.broadcasted_iota(jnp.int32, (1, tq), 1)
    cidx = lax.broadcasted_iota(jnp.int32, (n_ch, 1), 0)
    cmask = (cidx * CMP_STRIDE + (CMP_BLOCK - 1)) <= t
    blk = lax.broadcasted_iota(jnp.int32, (n_blk, tq), 0)
    cur = t >> 6
    forced = (blk == 0) | (blk == cur) | (blk == cur - 1)
    ratio = SEL_BLOCK // CMP_STRIDE

    def attend(nk):
        for h in range(NSA_HEADS):
            s_ref[h, 0:nk, :] = _dot(kc_ref[0, h // NSA_REP, 0:nk, :],
                                     qT_ref[0, HEAD_DIM * h:HEAD_DIM * (h + 1), :])
        heads = []
        for g in range(NSA_GROUPS):
            psum = jnp.zeros((nk, tq), F32)
            for r in range(NSA_REP):
                h = NSA_REP * g + r
                s = jnp.where(cmask[0:nk], s_ref[h, 0:nk, :], MASK_VAL)
                m = jnp.maximum(jnp.max(s, axis=0, keepdims=True), NEG_INF)
                e = jnp.exp2(s - m)
                den = jnp.sum(e, axis=0, keepdims=True)
                p = e * (1.0 / jnp.maximum(den, TINY))
                o = _dot(vcT_ref[0, g, :, 0:nk], p.astype(BF16))
                heads.append(o * gT_ref[0, 3 * h:3 * h + 1, :])
                psum = psum + p
            for j in range(tq // LANES):
                psum_ref[j, 0:nk, :] = psum[:, LANES * j:LANES * (j + 1)]
                if nk < n_ch:
                    psum_ref[j, nk:n_ch, :] = jnp.zeros((n_ch - nk, LANES), F32)
            rows = [jnp.concatenate([psum_ref[j, pl.ds(k, n_blk, stride=ratio), :] for j in range(tq // LANES)],
                                    axis=1) for k in range(ratio)]
            prev = jnp.where(blk == 0, 0.0, pltpu.roll(rows[ratio - 1], 1, 0))
            imp = prev + rows[0] + rows[1] + rows[2] + rows[3]
            imp_ref[g, 0:n_blk, :] = jnp.where(forced, FORCE, jnp.where(blk <= cur, imp, NEG_INF))
            if n_blk < MAX_SEL_BLOCKS:
                imp_ref[g, n_blk:MAX_SEL_BLOCKS, :] = jnp.full((MAX_SEL_BLOCKS - n_blk, tq), NEG_INF, F32)
            rank_ref[g] = jnp.zeros((MAX_SEL_BLOCKS, tq), jnp.int32)
        _store_head_pairs(o_ref, heads)

    early = pl.program_id(1) < pl.num_programs(1) // 2
    pl.when(early)(lambda: attend(n_ch // 2))
    pl.when(jnp.logical_not(early))(lambda: attend(n_ch))

    last_cur = (pl.program_id(1) * tq + tq - 1) >> 6
    sub = lax.broadcasted_iota(jnp.int32, (SUBLANES, tq), 0)
    n_groups = -(-n_blk // SUBLANES)
    for rg in range(n_groups):
        @pl.when(rg * SUBLANES <= last_cur)
        def _(rg=rg):
            for g in range(NSA_GROUPS):
                src = imp_ref[g, SUBLANES * rg:SUBLANES * (rg + 1), :]
                for v in range(n_groups):
                    tgt = imp_ref[g, SUBLANES * v:SUBLANES * (v + 1), :]
                    cnt = jnp.zeros((SUBLANES, tq), jnp.int32)
                    for r in range(SUBLANES):
                        row = src[r:r + 1, :]
                        if v > rg:
                            cnt = cnt + jnp.where(row >= tgt, 1, 0)
                        elif v < rg:
                            cnt = cnt + jnp.where(row > tgt, 1, 0)
                        else:
                            cnt = cnt + jnp.where(sub > r, jnp.where(row >= tgt, 1, 0), jnp.where(row > tgt, 1, 0))
                    rank_ref[g, SUBLANES * v:SUBLANES * (v + 1), :] += cnt
    for g in range(NSA_GROUPS):
        selb_ref[0, g] = jnp.where(rank_ref[g] < n_sel, 0.0, MASK_VAL).astype(BF16)


def _cmp_select(qnnT, k_c, v_cT, gnT):
    B, _, S = qnnT.shape
    tq = ATTN_TQ
    n_ch = k_c.shape[2]
    n_blk = S // SEL_BLOCK
    assert n_blk <= MAX_SEL_BLOCKS
    kern = functools.partial(_cmp_select_kernel, n_blk=n_blk, n_sel=min(N_SEL, n_blk))
    return pl.pallas_call(
        kern,
        grid=(B, S // tq),
        in_specs=[pl.BlockSpec((1, NSA_Q, tq), lambda b, i: (b, 0, i)),
                  pl.BlockSpec((1, NSA_GROUPS, n_ch, HEAD_DIM), lambda b, i: (b, 0, 0, 0)),
                  pl.BlockSpec((1, NSA_GROUPS, HEAD_DIM, n_ch), lambda b, i: (b, 0, 0, 0)),
                  pl.BlockSpec((1, GATE_ROWS, tq), lambda b, i: (b, 0, i))],
        out_specs=(pl.BlockSpec((1, tq, NSA_Q), lambda b, i: (b, i, 0)),
                   pl.BlockSpec((1, NSA_GROUPS, MAX_SEL_BLOCKS, tq), lambda b, i: (b, 0, 0, i))),
        out_shape=(jax.ShapeDtypeStruct((B, S, NSA_Q), BF16),
                   jax.ShapeDtypeStruct((B, NSA_GROUPS, MAX_SEL_BLOCKS, S), BF16)),
        scratch_shapes=[pltpu.VMEM((NSA_HEADS, n_ch, tq), F32),
                        pltpu.VMEM((tq // LANES, n_ch, LANES), F32),
                        pltpu.VMEM((NSA_GROUPS, MAX_SEL_BLOCKS, tq), F32),
                        pltpu.VMEM((NSA_GROUPS, MAX_SEL_BLOCKS, tq), jnp.int32)],
        compiler_params=pltpu.CompilerParams(dimension_semantics=("parallel", "parallel"),
                                             vmem_limit_bytes=VMEM_LIMIT),
        name="cmp_select",
    )(qnnT, k_c, v_cT, gnT)


def _attn_kernel(*refs, n_heads, rep, n_sub, nprev, selected, use_sink, gate_branch, interleave):
    refs = list(refs)
    qT_ref, k_ref, vT_ref = refs[:3]
    pos = 3
    gT_ref = selb_ref = e_ref = sink_ref = None
    if gate_branch is not None:
        gT_ref = refs[pos]; pos += 1
    if selected:
        selb_ref, e_ref = refs[pos], refs[pos + 1]; pos += 2
    if use_sink:
        sink_ref = refs[pos]; pos += 1
    o_ref = refs[pos]
    rhs_ref, sa_ref, sb_ref, acc_ref, m_ref = refs[pos + 1:pos + 6]

    tq = sa_ref.shape[1]
    ksub = lax.broadcasted_iota(jnp.int32, (tq, tq), 0)
    qlane = lax.broadcasted_iota(jnp.int32, (tq, tq), 1)
    zeros_h = jnp.zeros((HEAD_DIM, tq), BF16)
    ones_rows = jnp.ones((DEN_ROWS, tq), BF16)
    acc_rows = HEAD_DIM + DEN_ROWS
    den_row = lax.broadcasted_iota(jnp.int32, (acc_rows, tq), 0) >= HEAD_DIM
    groups = sorted({h // rep for h in range(n_heads)})

    def qk_head(c, h, dst_ref):
        start = pl.multiple_of(c * tq, tq)
        k = k_ref[0, pl.ds(start, tq), :]
        if selected:
            k = jnp.concatenate([k, e_ref[pl.ds(start, tq), :]], axis=1)
        dst_ref[h] = _dot(k, rhs_ref[h])

    def softmax_pv_head(c, h, src_ref, mask, bias):
        g = h // rep
        v_aug = jnp.concatenate([vT_ref[0, c, HEAD_DIM * g:HEAD_DIM * (g + 1), :], ones_rows], axis=0)
        s = src_ref[h]
        if bias is not None:
            s = s + bias
        if mask is not None:
            s = jnp.where(mask, s, MASK_VAL)
        m_old = m_ref[h]
        m_new = jnp.maximum(m_old, jnp.max(s, axis=0, keepdims=True))
        p = jnp.exp2((s - m_new).astype(BF16))
        acc_ref[h] = jnp.exp2(m_old - m_new) * acc_ref[h] + _dot(v_aug, p)
        m_ref[h] = m_new

    def qk(c, dst_ref):
        for h in range(n_heads):
            qk_head(c, h, dst_ref)

    def softmax_pv(c, src_ref, mask=None, bias=None, next_c=None, next_ref=None):
        if next_ref is not None and not interleave:
            qk(next_c, next_ref)
        for h in range(n_heads):
            if next_ref is not None and interleave:
                qk_head(next_c, h, next_ref)
            softmax_pv_head(c, h, src_ref, mask, bias)

    for sub in range(n_sub):
        i = pl.program_id(1) * n_sub + sub
        cols = slice(sub * tq, (sub + 1) * tq)
        for h in range(n_heads):
            g = h // rep
            parts = [zeros_h, zeros_h]
            parts[g] = qT_ref[0, HEAD_DIM * h:HEAD_DIM * (h + 1), cols]
            if selected:
                parts += [selb_ref[0, g, :, cols], zeros_h]
            rhs_ref[h] = jnp.concatenate(parts, axis=0)
            if use_sink:
                m_ref[h] = jnp.full((1, tq), sink_ref[h] * LOG2E, F32)
                acc_ref[h] = jnp.where(den_row, 1.0, 0.0)
            else:
                m_ref[h] = jnp.full((1, tq), NEG_INF, F32)
                acc_ref[h] = jnp.zeros((acc_rows, tq), F32)

        if selected:
            qk(0, sa_ref)

            def pair(j, carry):
                c = 2 * j
                softmax_pv(c, sa_ref, next_c=c + 1, next_ref=sb_ref)
                softmax_pv(c + 1, sb_ref, next_c=c + 2, next_ref=sa_ref)
                return carry

            lax.fori_loop(0, i // 2, pair, 0)
            odd = i % 2
            last = i - odd

            softmax_pv(last, sa_ref, mask=ksub <= qlane + odd * tq, next_c=i, next_ref=sb_ref)

            @pl.when(odd == 1)
            def _():
                softmax_pv(i, sb_ref, mask=ksub <= qlane)
        else:
            bufs = (sa_ref, sb_ref)
            chunks = [jnp.maximum(i - d, 0) for d in range(nprev, 0, -1)] + [i]
            qk(chunks[0], bufs[0])
            for n, d in enumerate(range(nprev, -1, -1)):
                nxt = dict(next_c=chunks[n + 1], next_ref=bufs[(n + 1) % 2]) if n + 1 < len(chunks) else {}
                if d == 0:
                    softmax_pv(chunks[n], bufs[n % 2], mask=ksub <= qlane, **nxt)
                elif d == nprev:
                    gone = jnp.where(i >= d, 0, tq)
                    softmax_pv(chunks[n], bufs[n % 2], mask=ksub > qlane + gone, **nxt)
                else:
                    softmax_pv(chunks[n], bufs[n % 2], bias=jnp.where(i >= d, 0.0, MASK_VAL), **nxt)

        heads = []
        for h in range(n_heads):
            acc = acc_ref[h]
            o = acc[0:HEAD_DIM] * (1.0 / jnp.maximum(acc[HEAD_DIM:HEAD_DIM + 1], TINY))
            if gate_branch is not None:
                o = o * gT_ref[0, 3 * h + gate_branch:3 * h + gate_branch + 1, cols]
            heads.append(o)
        _store_head_pairs(o_ref, heads, cols)


def _attention(qT, k, vT, *, tq, rep, n_sub=1, nprev=0, selected=False, gnT=None, gate_branch=None,
               selb=None, sinks=None, interleave=True):
    B, F, S = qT.shape
    n_heads = F // HEAD_DIM
    n_kv = vT.shape[2]
    tstep = tq * n_sub
    assert vT.shape == (B, S // tq, n_kv, tq) and S % tstep == 0
    in_specs = [pl.BlockSpec((1, F, tstep), lambda b, i: (b, 0, i)),
                pl.BlockSpec((1, S, LANES), lambda b, i: (b, 0, 0)),
                pl.BlockSpec((1, S // tq, n_kv, tq), lambda b, i: (b, 0, 0, 0))]
    args = [qT, k, vT]
    if gate_branch is not None:
        in_specs.append(pl.BlockSpec((1, GATE_ROWS, tstep), lambda b, i: (b, 0, i)))
        args.append(gnT)
    if selected:
        onehot = (jnp.arange(S)[:, None] // SEL_BLOCK == jnp.arange(LANES)[None, :]).astype(BF16)
        in_specs += [pl.BlockSpec((1, NSA_GROUPS, MAX_SEL_BLOCKS, tstep), lambda b, i: (b, 0, 0, i)),
                     pl.BlockSpec((S, LANES), lambda b, i: (0, 0))]
        args += [selb, onehot]
    if sinks is not None:
        in_specs.append(pl.BlockSpec(memory_space=pltpu.SMEM))
        args.append(sinks)
    kern = functools.partial(_attn_kernel, n_heads=n_heads, rep=rep, n_sub=n_sub, nprev=nprev,
                             selected=selected, use_sink=sinks is not None, gate_branch=gate_branch,
                             interleave=interleave)
    contraction = 2 * LANES if selected else LANES
    return pl.pallas_call(
        kern,
        grid=(B, S // tstep),
        in_specs=in_specs,
        out_specs=pl.BlockSpec((1, tstep, F), lambda b, i: (b, i, 0)),
        out_shape=jax.ShapeDtypeStruct((B, S, F), BF16),
        scratch_shapes=[pltpu.VMEM((n_heads, contraction, tq), BF16),
                        pltpu.VMEM((n_heads, tq, tq), F32),
                        pltpu.VMEM((n_heads, tq, tq), F32),
                        pltpu.VMEM((n_heads, HEAD_DIM + DEN_ROWS, tq), F32),
                        pltpu.VMEM((n_heads, 1, tq), F32)],
        compiler_params=pltpu.CompilerParams(dimension_semantics=("parallel", "parallel"),
                                             vmem_limit_bytes=VMEM_LIMIT),
        name="attn_sel" if selected else ("attn_swa" if sinks is not None else "attn_win"),
    )(*args)


def _layer_norm(r, g, b):
    mu = jnp.mean(r, axis=-1, keepdims=True)
    d = r - mu
    var = jnp.mean(d * d, axis=-1, keepdims=True)
    return d * lax.rsqrt(var + LN_EPS) * g + b


def _mlp_kernel(x_ref, oa_ref, oc_ref, os_ref, ow_ref, gm_ref, p_ref, wa_ref, wb_ref, wo_ref, g1_ref, b1_ref,
                wg_ref, wu_ref, wd_ref, wpg_ref, wple_ref, g2_ref, b2_ref, o_ref, *, alpha):
    tm = x_ref.shape[0]
    blocks = [slice(r * tm // MLP_SPLIT, (r + 1) * tm // MLP_SPLIT) for r in range(MLP_SPLIT)]

    def merge(r):
        o_b = (oc_ref[r, :].astype(F32) + os_ref[r, :].astype(F32) + ow_ref[r, :].astype(F32)).astype(BF16)
        return (gm_ref[r, 0:D_MODEL].astype(F32) * _dot(oa_ref[r, :], wa_ref[...])
                + gm_ref[r, D_MODEL:2 * D_MODEL].astype(F32) * _dot(o_b, wb_ref[...]))

    def norm1(r, y):
        return _layer_norm(alpha * x_ref[r, :] + _dot(y.astype(BF16), wo_ref[...]), g1_ref[...], b1_ref[...])

    def swiglu(h):
        hb = h.astype(BF16)
        ff = jnp.zeros(h.shape, F32)
        for lo, hi in FF_CHUNKS:
            sl = slice(lo, hi)
            a = jax.nn.silu(_dot(hb, wg_ref[:, sl])) * _dot(hb, wu_ref[:, sl])
            ff = ff + _dot(a.astype(BF16), wd_ref[sl, :])
        return ff

    def norm2(r, h, ff):
        ple = jax.nn.sigmoid(_dot(h.astype(BF16), wpg_ref[...])) * _dot(p_ref[r, :].astype(BF16), wple_ref[...])
        o_ref[r, :] = _layer_norm(alpha * h + ff + ple, g2_ref[...], b2_ref[...])

    ys = [merge(r) for r in blocks]
    hs = [norm1(r, y) for r, y in zip(blocks, ys)]
    ffs = [swiglu(h) for h in hs]
    for r, h, ff in zip(blocks, hs, ffs):
        norm2(r, h, ff)


def _mlp(x2, o_a, o_c, o_s, o_w, gm, p2, w_a, w_b, w_o, ln1_g, ln1_b, w_g, w_u, w_d, w_pg, w_ple, ln2_g, ln2_b,
         alpha):
    T, D = x2.shape
    tm = MLP_TM
    row = lambda w: pl.BlockSpec((tm, w), lambda i: (i, 0))
    const = lambda shp: pl.BlockSpec(shp, lambda i: (0, 0), pipeline_mode=pl.Buffered(1))
    bf = lambda w: w.astype(BF16)
    return pl.pallas_call(
        functools.partial(_mlp_kernel, alpha=alpha),
        grid=(T // tm,),
        in_specs=[row(D), row(SWA_Q), row(NSA_Q), row(NSA_Q), row(NSA_Q), row(2 * D), row(PLE_DIM),
                  const((SWA_Q, D)), const((NSA_Q, D)), const((D, D)), const((1, D)), const((1, D)),
                  const((D, D_FF)), const((D, D_FF)), const((D_FF, D)), const((D, D)), const((PLE_DIM, D)),
                  const((1, D)), const((1, D))],
        out_specs=row(D),
        out_shape=jax.ShapeDtypeStruct((T, D), F32),
        compiler_params=pltpu.CompilerParams(dimension_semantics=("parallel",),
                                             vmem_limit_bytes=MLP_VMEM_LIMIT),
        name="mlp_ln",
    )(x2, o_a, o_c, o_s, o_w, gm, p2, bf(w_a), bf(w_b), bf(w_o), ln1_g[None, :], ln1_b[None, :],
      bf(w_g), bf(w_u), bf(w_d), bf(w_pg), bf(w_ple), ln2_g[None, :], ln2_b[None, :])


def kernel(x, p, positions, w_in, attn_sinks, cmp_pos_emb, w_cmp_k1, w_cmp_k2, w_cmp_v1, w_cmp_v2, w_proj_swa, w_proj_nsa, w_out, ln1_g, ln1_b, w_ff_gate, w_ff_up, w_ff_down, w_ple, w_ple_gate, ln2_g, ln2_b):
    B, S, D = x.shape
    depth = w_in.shape[0]
    assert D == D_MODEL and S % INPROJ_TM == 0 and S >= NSA_WINDOW + ATTN_TQ
    alpha = (2.0 * depth) ** 0.25
    h = x
    for i in range(depth):
        (ks, kw, ka, kc, vc, gm, qaT, qnrT, qnnT, vsT, vwT, vaT, gnT) = _inproj(h, positions, w_in[i])
        k_c, v_cT = _compress(kc, vc, cmp_pos_emb[i], w_cmp_k1[i], w_cmp_k2[i], w_cmp_v1[i], w_cmp_v2[i])
        o_cmp, selb = _cmp_select(qnnT, k_c, v_cT, gnT)
        o_slc = _attention(qnrT, ks, vsT, tq=SEL_TQ, rep=NSA_REP, selected=True, gnT=gnT, gate_branch=1,
                           selb=selb)
        o_win = _attention(qnrT, kw, vwT, tq=ATTN_TQ, rep=NSA_REP, nprev=NSA_WINDOW // ATTN_TQ, gnT=gnT,
                           gate_branch=2)
        o_swa = _attention(qaT, ka, vaT, tq=SWA_TQ, rep=SWA_HEADS, n_sub=SWA_SUBTILES,
                           nprev=SWA_WINDOW // SWA_TQ, sinks=attn_sinks[i].astype(F32), interleave=False)
        flat = lambda a: a.reshape(B * S, a.shape[-1])
        h2 = _mlp(flat(h), flat(o_swa), flat(o_cmp), flat(o_slc), flat(o_win), flat(gm), flat(p[i]),
                  w_proj_swa[i], w_proj_nsa[i], w_out[i], ln1_g[i], ln1_b[i],
                  w_ff_gate[i], w_ff_up[i], w_ff_down[i], w_ple_gate[i], w_ple[i], ln2_g[i], ln2_b[i], alpha)
        h = h2.reshape(B, S, D)
    return h
```

```python
import functools

import jax
import jax.numpy as jnp
from jax import lax
from jax.experimental import pallas as pl
from jax.experimental.pallas import tpu as pltpu

F32 = jnp.float32
BF16 = jnp.bfloat16

D_MODEL = 1024
HEAD_DIM = 64
HALF_DIM = HEAD_DIM // 2
ROPE_THETA = 10000.0
LN_EPS = 1e-5
NEG_INF = -1e30
FORCE = 1e30
TINY = 1e-30
PLE_DIM = 256

SWA_HEADS = 8
SWA_WINDOW = 128
NSA_HEADS = 8
NSA_GROUPS = 2
NSA_REP = NSA_HEADS // NSA_GROUPS
NSA_WINDOW = 512
CMP_BLOCK = 32
CMP_STRIDE = 16
CMP_HIDDEN = 256
SEL_BLOCK = 64
N_SEL = 16
D_FF = 2816

SWA_Q = SWA_HEADS * HEAD_DIM
NSA_Q = NSA_HEADS * HEAD_DIM
NSA_KV = NSA_GROUPS * HEAD_DIM
NSA_GATES = NSA_HEADS * 3
GATE_ROWS = 32
MAX_SEL_BLOCKS = 64
DEN_ROWS = 16
LOG2E = 1.4426950408889634

LANES = 128
VMEM_LIMIT = 56 * 1024 * 1024
MLP_VMEM_LIMIT = 60 * 1024 * 1024

TOK_KV_COLS = 2 * LANES
TOK_COLS = TOK_KV_COLS + 2 * D_MODEL
FEAT_K_ROWS = 3 * LANES
FEAT_V_ROWS = 2 * NSA_KV + HEAD_DIM + GATE_ROWS + 32
FEAT_ROWS = SWA_Q + NSA_Q + FEAT_K_ROWS + FEAT_V_ROWS

INPROJ_TM = 512
ATTN_TQ = 256
SEL_TQ = 256
SWA_TQ = 128
SWA_SUBTILES = 2
QK_LOOKAHEAD = 7
MLP_TM = 512
MLP_SPLIT = 2
FF_CHUNKS = ((0, 1536), (1536, D_FF))
MASK_VAL = 2.0 * NEG_INF
SUBLANES = 8


def _dot(a, b):
    return jnp.dot(a, b, preferred_element_type=F32)


def _dot_nt(a, b):
    return lax.dot_general(a, b, (((1,), (1,)), ((), ())), preferred_element_type=F32)


def _inproj_kernel(x_ref, posr_ref, invc_ref, wn_ref, wt_ref,
                   ks_ref, kw_ref, ka_ref, kc_ref, vc_ref, gm_ref,
                   qaT_ref, qnrT_ref, qnnT_ref, vsT_ref, vwT_ref, vaT_ref, gnT_ref):
    tm = x_ref.shape[1]
    xb = x_ref[0].astype(BF16)

    zkv = _dot(xb, wn_ref[:, 0:TOK_KV_COLS])
    kc_ref[0] = zkv[:, 0:LANES]
    vc_ref[0] = zkv[:, LANES:2 * LANES]
    gchunk = 512
    for c in range(2 * D_MODEL // gchunk):
        zg = _dot(xb, wn_ref[:, TOK_KV_COLS + c * gchunk:TOK_KV_COLS + (c + 1) * gchunk])
        gm_ref[0, :, c * gchunk:(c + 1) * gchunk] = jax.nn.sigmoid(zg).astype(BF16)

    ang_t = invc_ref[...] * posr_ref[0]
    cos_f = jnp.cos(ang_t)
    sin_f = jnp.sin(ang_t)
    scale = HEAD_DIM ** -0.5 * LOG2E

    def rope_feat(z, n_heads, mul):
        out = []
        for h in range(n_heads):
            x1 = z[HEAD_DIM * h:HEAD_DIM * h + HALF_DIM]
            x2 = z[HEAD_DIM * h + HALF_DIM:HEAD_DIM * (h + 1)]
            out += [(x1 * cos_f - x2 * sin_f) * mul, (x2 * cos_f + x1 * sin_f) * mul]
        return out

    def store_rows(out_ref, pieces):
        for n, piece in enumerate(pieces):
            out_ref[0, HALF_DIM * n:HALF_DIM * (n + 1), :] = piece.astype(BF16)

    zqa = _dot_nt(wt_ref[0:SWA_Q, :], xb)
    store_rows(qaT_ref, rope_feat(zqa, SWA_HEADS, scale))
    zqn = _dot_nt(wt_ref[SWA_Q:SWA_Q + NSA_Q, :], xb)
    store_rows(qnrT_ref, rope_feat(zqn, NSA_HEADS, scale))
    qnnT_ref[0] = (zqn * scale).astype(BF16)
    k0 = SWA_Q + NSA_Q
    zk = _dot_nt(wt_ref[k0:k0 + FEAT_K_ROWS, :], xb)
    kr = rope_feat(zk, FEAT_K_ROWS // HEAD_DIM, 1.0)
    for j, out_ref in enumerate((ks_ref, kw_ref, ka_ref)):
        blk = jnp.concatenate(kr[4 * j:4 * (j + 1)], axis=0)
        out_ref[0] = blk.T.astype(BF16)
    zv = _dot_nt(wt_ref[k0 + FEAT_K_ROWS:FEAT_ROWS, :], xb)
    for j in range(tm // SEL_TQ):
        sl = slice(j * SEL_TQ, (j + 1) * SEL_TQ)
        vsT_ref[0, j] = zv[0:NSA_KV, sl].astype(BF16)
    for j in range(tm // ATTN_TQ):
        sl = slice(j * ATTN_TQ, (j + 1) * ATTN_TQ)
        vwT_ref[0, j] = zv[NSA_KV:2 * NSA_KV, sl].astype(BF16)
    for j in range(tm // SWA_TQ):
        sl = slice(j * SWA_TQ, (j + 1) * SWA_TQ)
        vaT_ref[0, j] = zv[2 * NSA_KV:2 * NSA_KV + HEAD_DIM, sl].astype(BF16)
    g0 = 2 * NSA_KV + HEAD_DIM
    gnT_ref[0] = jax.nn.sigmoid(zv[g0:g0 + GATE_ROWS])


def _inproj(x, positions, w_in):
    B, S, D = x.shape
    tm = INPROJ_TM
    offs = {}
    off = 0
    for name, n in (("qa", SWA_Q), ("ka", HEAD_DIM), ("va", HEAD_DIM), ("qn", NSA_Q), ("kc", NSA_KV),
                    ("vc", NSA_KV), ("ks", NSA_KV), ("vs", NSA_KV), ("kw", NSA_KV), ("vw", NSA_KV),
                    ("gn", NSA_GATES), ("gm", 2 * D_MODEL)):
        offs[name] = (off, off + n)
        off += n
    col = lambda name: w_in[:, offs[name][0]:offs[name][1]]
    zpad = lambda n: jnp.zeros((D, n), w_in.dtype)
    wn = jnp.concatenate([col("kc"), col("vc"), col("gm")], axis=1).astype(BF16)
    wt = jnp.concatenate([col("qa"), col("qn"), col("ks"), col("kw"), col("ka"), zpad(LANES - HEAD_DIM),
                          col("vs"), col("vw"), col("va"), col("gn"),
                          zpad(FEAT_V_ROWS - (2 * NSA_KV + HEAD_DIM + NSA_GATES))], axis=1).T.astype(BF16)
    assert wn.shape == (D, TOK_COLS) and wt.shape == (FEAT_ROWS, D)

    inv_col = (ROPE_THETA ** (-jnp.arange(0, HEAD_DIM, 2, dtype=F32) / HEAD_DIM))[:, None]
    pos_row = positions.astype(F32)[:, None, :]

    tok = lambda w: pl.BlockSpec((1, tm, w), lambda b, i: (b, i, 0))
    feat = lambda r: pl.BlockSpec((1, r, tm), lambda b, i: (b, 0, i))
    const = lambda shp: pl.BlockSpec(shp, lambda b, i: (0,) * len(shp))
    out_shape = (
        jax.ShapeDtypeStruct((B, S, LANES), BF16),
        jax.ShapeDtypeStruct((B, S, LANES), BF16),
        jax.ShapeDtypeStruct((B, S, LANES), BF16),
        jax.ShapeDtypeStruct((B, S, LANES), F32),
        jax.ShapeDtypeStruct((B, S, LANES), F32),
        jax.ShapeDtypeStruct((B, S, 2 * D_MODEL), BF16),
        jax.ShapeDtypeStruct((B, SWA_Q, S), BF16),
        jax.ShapeDtypeStruct((B, NSA_Q, S), BF16),
        jax.ShapeDtypeStruct((B, NSA_Q, S), BF16),
        jax.ShapeDtypeStruct((B, S // SEL_TQ, NSA_KV, SEL_TQ), BF16),
        jax.ShapeDtypeStruct((B, S // ATTN_TQ, NSA_KV, ATTN_TQ), BF16),
        jax.ShapeDtypeStruct((B, S // SWA_TQ, HEAD_DIM, SWA_TQ), BF16),
        jax.ShapeDtypeStruct((B, GATE_ROWS, S), F32),
    )
    out_specs = (
        tok(LANES), tok(LANES), tok(LANES), tok(LANES), tok(LANES), tok(2 * D_MODEL),
        feat(SWA_Q), feat(NSA_Q), feat(NSA_Q),
        pl.BlockSpec((1, tm // SEL_TQ, NSA_KV, SEL_TQ), lambda b, i: (b, i, 0, 0)),
        pl.BlockSpec((1, tm // ATTN_TQ, NSA_KV, ATTN_TQ), lambda b, i: (b, i, 0, 0)),
        pl.BlockSpec((1, tm // SWA_TQ, HEAD_DIM, SWA_TQ), lambda b, i: (b, i, 0, 0)),
        feat(GATE_ROWS),
    )
    return pl.pallas_call(
        _inproj_kernel,
        grid=(B, S // tm),
        in_specs=[tok(D), pl.BlockSpec((1, 1, tm), lambda b, i: (b, 0, i)),
                  const((HALF_DIM, 1)), const((D, TOK_COLS)), const((FEAT_ROWS, D))],
        out_specs=out_specs,
        out_shape=out_shape,
        compiler_params=pltpu.CompilerParams(dimension_semantics=("parallel", "parallel"),
                                             vmem_limit_bytes=VMEM_LIMIT),
        name="inproj",
    )(x, pos_row, inv_col, wn, wt)


def _gelu_tanh(x):
    return 0.5 * x * (1.0 + jnp.tanh(0.7978845608028654 * (x + 0.044715 * (x * x * x))))


def _compress_kernel(kc_ref, vc_ref, pe_ref, wbk_ref, w1k_ref, w2k_ref, wbv_ref, w1v_ref, w2vT_ref,
                     kcmp_ref, vcmpT_ref):
    n_ch = kc_ref.shape[1] // CMP_STRIDE

    def hidden(x_ref, wb_ref, w1_ref, pe_row):
        acc = jnp.zeros((n_ch, 2 * NSA_GROUPS * CMP_HIDDEN), F32)
        for t in range(CMP_STRIDE):
            x_t = x_ref[0, pl.ds(t, n_ch, stride=CMP_STRIDE), :].astype(BF16)
            acc = acc + _dot(x_t, wb_ref[t])
        bias = _dot(jnp.broadcast_to(pe_row, (SUBLANES, pe_row.shape[1])).astype(BF16), w1_ref[...])[0:1]
        out = []
        for g in range(NSA_GROUPS):
            u = acc[:, 2 * CMP_HIDDEN * g:2 * CMP_HIDDEN * g + CMP_HIDDEN]
            v = acc[:, 2 * CMP_HIDDEN * g + CMP_HIDDEN:2 * CMP_HIDDEN * (g + 1)]
            out.append(_gelu_tanh(u + pltpu.roll(v, n_ch - 1, 0) + bias).astype(BF16))
        return out

    hk = hidden(kc_ref, wbk_ref, w1k_ref, pe_ref[0:1, :])
    hv = hidden(vc_ref, wbv_ref, w1v_ref, pe_ref[1:2, :])
    for g in range(NSA_GROUPS):
        kcmp_ref[0, g] = _dot(hk[g], w2k_ref[...]).astype(BF16)
        vcmpT_ref[0, g] = _dot_nt(w2vT_ref[...], hv[g]).astype(BF16)


def _compress(kc, vc, pos_emb, w_k1, w_k2, w_v1, w_v2):
    B, S, _ = kc.shape
    n_ch = S // CMP_STRIDE
    half = CMP_STRIDE * HEAD_DIM

    def per_token(w1):
        blk = jnp.concatenate([w1[:half].reshape(CMP_STRIDE, HEAD_DIM, CMP_HIDDEN),
                               w1[half:].reshape(CMP_STRIDE, HEAD_DIM, CMP_HIDDEN)], axis=2)
        z = jnp.zeros_like(blk)
        return jnp.concatenate([jnp.concatenate([blk, z], axis=2), jnp.concatenate([z, blk], axis=2)],
                               axis=1).astype(BF16)

    pe = pos_emb.reshape(2, 2 * half)
    const = lambda shp: pl.BlockSpec(shp, lambda b: (0,) * len(shp))
    wb_shape = (CMP_STRIDE, NSA_KV, 2 * NSA_GROUPS * CMP_HIDDEN)
    return pl.pallas_call(
        _compress_kernel,
        grid=(B,),
        in_specs=[pl.BlockSpec((1, S, NSA_KV), lambda b: (b, 0, 0)),
                  pl.BlockSpec((1, S, NSA_KV), lambda b: (b, 0, 0)),
                  const((2, 2 * half)),
                  const(wb_shape), const((2 * half, CMP_HIDDEN)), const((CMP_HIDDEN, HEAD_DIM)),
                  const(wb_shape), const((2 * half, CMP_HIDDEN)), const((HEAD_DIM, CMP_HIDDEN))],
        out_specs=(pl.BlockSpec((1, NSA_GROUPS, n_ch, HEAD_DIM), lambda b: (b, 0, 0, 0)),
                   pl.BlockSpec((1, NSA_GROUPS, HEAD_DIM, n_ch), lambda b: (b, 0, 0, 0))),
        out_shape=(jax.ShapeDtypeStruct((B, NSA_GROUPS, n_ch, HEAD_DIM), BF16),
                   jax.ShapeDtypeStruct((B, NSA_GROUPS, HEAD_DIM, n_ch), BF16)),
        compiler_params=pltpu.CompilerParams(dimension_semantics=("parallel",),
                                             vmem_limit_bytes=VMEM_LIMIT),
        name="compress",
    )(kc, vc, pe, per_token(w_k1), w_k1.astype(BF16), w_k2.astype(BF16),
      per_token(w_v1), w_v1.astype(BF16), w_v2.T.astype(BF16))


def _store_head_pairs(o_ref, heads, rows=slice(None)):
    for pr in range(len(heads) // 2):
        pair = jnp.concatenate([heads[2 * pr], heads[2 * pr + 1]], axis=0)
        o_ref[0, rows, LANES * pr:LANES * (pr + 1)] = pair.T.astype(BF16)


def _cmp_select_kernel(qT_ref, kc_ref, vcT_ref, gT_ref, o_ref, selb_ref, s_ref, psum_ref, imp_ref, rank_ref, *,
                       n_blk, n_sel):
    tq = qT_ref.shape[2]
    n_ch = kc_ref.shape[2]
    t = pl.program_id(1) * tq + lax.broadcasted_iota(jnp.int32, (1, tq), 1)
    cidx = lax.broadcasted_iota(jnp.int32, (n_ch, 1), 0)
    cmask = (cidx * CMP_STRIDE + (CMP_BLOCK - 1)) <= t
    blk = lax.broadcasted_iota(jnp.int32, (n_blk, tq), 0)
    cur = t >> 6
    forced = (blk == 0) | (blk == cur) | (blk == cur - 1)
    ratio = SEL_BLOCK // CMP_STRIDE

    def attend(nk):
        for h in range(NSA_HEADS):
            s_ref[h, 0:nk, :] = _dot(kc_ref[0, h // NSA_REP, 0:nk, :],
                                     qT_ref[0, HEAD_DIM * h:HEAD_DIM * (h + 1), :])
        heads = []
        for g in range(NSA_GROUPS):
            psum = jnp.zeros((nk, tq), F32)
            for r in range(NSA_REP):
                h = NSA_REP * g + r
                s = jnp.where(cmask[0:nk], s_ref[h, 0:nk, :], MASK_VAL)
                m = jnp.maximum(jnp.max(s, axis=0, keepdims=True), NEG_INF)
                e = jnp.exp2(s - m)
                den = jnp.sum(e, axis=0, keepdims=True)
                p = e * (1.0 / jnp.maximum(den, TINY))
                o = _dot(vcT_ref[0, g, :, 0:nk], p.astype(BF16))
                heads.append(o * gT_ref[0, 3 * h:3 * h + 1, :])
                psum = psum + p
            for j in range(tq // LANES):
                psum_ref[j, 0:nk, :] = psum[:, LANES * j:LANES * (j + 1)]
                if nk < n_ch:
                    psum_ref[j, nk:n_ch, :] = jnp.zeros((n_ch - nk, LANES), F32)
            rows = [jnp.concatenate([psum_ref[j, pl.ds(k, n_blk, stride=ratio), :] for j in range(tq // LANES)],
                                    axis=1) for k in range(ratio)]
            prev = jnp.where(blk == 0, 0.0, pltpu.roll(rows[ratio - 1], 1, 0))
            imp = prev + rows[0] + rows[1] + rows[2] + rows[3]
            imp_ref[g, 0:n_blk, :] = jnp.where(forced, FORCE, jnp.where(blk <= cur, imp, NEG_INF))
            if n_blk < MAX_SEL_BLOCKS:
                imp_ref[g, n_blk:MAX_SEL_BLOCKS, :] = jnp.full((MAX_SEL_BLOCKS - n_blk, tq), NEG_INF, F32)
            rank_ref[g] = jnp.zeros((MAX_SEL_BLOCKS, tq), jnp.int32)
        _store_head_pairs(o_ref, heads)

    early = pl.program_id(1) < pl.num_programs(1) // 2
    pl.when(early)(lambda: attend(n_ch // 2))
    pl.when(jnp.logical_not(early))(lambda: attend(n_ch))

    last_cur = (pl.program_id(1) * tq + tq - 1) >> 6
    sub = lax.broadcasted_iota(jnp.int32, (SUBLANES, tq), 0)
    n_groups = -(-n_blk // SUBLANES)
    for rg in range(n_groups):
        @pl.when(rg * SUBLANES <= last_cur)
        def _(rg=rg):
            for g in range(NSA_GROUPS):
                src = imp_ref[g, SUBLANES * rg:SUBLANES * (rg + 1), :]
                for v in range(n_groups):
                    tgt = imp_ref[g, SUBLANES * v:SUBLANES * (v + 1), :]
                    cnt = jnp.zeros((SUBLANES, tq), jnp.int32)
                    for r in range(SUBLANES):
                        row = src[r:r + 1, :]
                        if v > rg:
                            cnt = cnt + jnp.where(row >= tgt, 1, 0)
                        elif v < rg:
                            cnt = cnt + jnp.where(row > tgt, 1, 0)
                        else:
                            cnt = cnt + jnp.where(sub > r, jnp.where(row >= tgt, 1, 0), jnp.where(row > tgt, 1, 0))
                    rank_ref[g, SUBLANES * v:SUBLANES * (v + 1), :] += cnt
    for g in range(NSA_GROUPS):
        selb_ref[0, g] = jnp.where(rank_ref[g] < n_sel, 0.0, MASK_VAL).astype(BF16)


def _cmp_select(qnnT, k_c, v_cT, gnT):
    B, _, S = qnnT.shape
    tq = ATTN_TQ
    n_ch = k_c.shape[2]
    n_blk = S // SEL_BLOCK
    assert n_blk <= MAX_SEL_BLOCKS
    kern = functools.partial(_cmp_select_kernel, n_blk=n_blk, n_sel=min(N_SEL, n_blk))
    return pl.pallas_call(
        kern,
        grid=(B, S // tq),
        in_specs=[pl.BlockSpec((1, NSA_Q, tq), lambda b, i: (b, 0, i)),
                  pl.BlockSpec((1, NSA_GROUPS, n_ch, HEAD_DIM), lambda b, i: (b, 0, 0, 0)),
                  pl.BlockSpec((1, NSA_GROUPS, HEAD_DIM, n_ch), lambda b, i: (b, 0, 0, 0)),
                  pl.BlockSpec((1, GATE_ROWS, tq), lambda b, i: (b, 0, i))],
        out_specs=(pl.BlockSpec((1, tq, NSA_Q), lambda b, i: (b, i, 0)),
                   pl.BlockSpec((1, NSA_GROUPS, MAX_SEL_BLOCKS, tq), lambda b, i: (b, 0, 0, i))),
        out_shape=(jax.ShapeDtypeStruct((B, S, NSA_Q), BF16),
                   jax.ShapeDtypeStruct((B, NSA_GROUPS, MAX_SEL_BLOCKS, S), BF16)),
        scratch_shapes=[pltpu.VMEM((NSA_HEADS, n_ch, tq), F32),
                        pltpu.VMEM((tq // LANES, n_ch, LANES), F32),
                        pltpu.VMEM((NSA_GROUPS, MAX_SEL_BLOCKS, tq), F32),
                        pltpu.VMEM((NSA_GROUPS, MAX_SEL_BLOCKS, tq), jnp.int32)],
        compiler_params=pltpu.CompilerParams(dimension_semantics=("parallel", "parallel"),
                                             vmem_limit_bytes=VMEM_LIMIT),
        name="cmp_select",
    )(qnnT, k_c, v_cT, gnT)


def _attn_kernel(*refs, n_heads, rep, n_sub, nprev, selected, use_sink, gate_branch, lookahead):
    refs = list(refs)
    qT_ref, k_ref, vT_ref = refs[:3]
    pos = 3
    gT_ref = selb_ref = e_ref = sink_ref = None
    if gate_branch is not None:
        gT_ref = refs[pos]; pos += 1
    if selected:
        selb_ref, e_ref = refs[pos], refs[pos + 1]; pos += 2
    if use_sink:
        sink_ref = refs[pos]; pos += 1
    o_ref = refs[pos]
    rhs_ref, acc_ref, m_ref, sa_ref = refs[pos + 1:pos + 5]
    sb_ref = None if lookahead else refs[pos + 5]

    tq = sa_ref.shape[1]
    ksub = lax.broadcasted_iota(jnp.int32, (tq, tq), 0)
    qlane = lax.broadcasted_iota(jnp.int32, (tq, tq), 1)
    zeros_h = jnp.zeros((HEAD_DIM, tq), BF16)
    ones_rows = jnp.ones((DEN_ROWS, tq), BF16)
    acc_rows = HEAD_DIM + DEN_ROWS
    den_row = lax.broadcasted_iota(jnp.int32, (acc_rows, tq), 0) >= HEAD_DIM
    groups = sorted({h // rep for h in range(n_heads)})

    def qk_head(c, h, dst_ref):
        start = pl.multiple_of(c * tq, tq)
        k = k_ref[0, pl.ds(start, tq), :]
        if selected:
            k = jnp.concatenate([k, e_ref[pl.ds(start, tq), :]], axis=1)
        dst_ref[h] = _dot(k, rhs_ref[h])

    def softmax_pv_head(c, h, src_ref, mask, bias):
        g = h // rep
        v_aug = jnp.concatenate([vT_ref[0, c, HEAD_DIM * g:HEAD_DIM * (g + 1), :], ones_rows], axis=0)
        s = src_ref[h]
        if bias is not None:
            s = s + bias
        if mask is not None:
            s = jnp.where(mask, s, MASK_VAL)
        m_old = m_ref[h]
        m_new = jnp.maximum(m_old, jnp.max(s, axis=0, keepdims=True))
        p = jnp.exp2((s - m_new).astype(BF16))
        acc_ref[h] = jnp.exp2(m_old - m_new) * acc_ref[h] + _dot(v_aug, p)
        m_ref[h] = m_new

    def qk(c, dst_ref):
        for h in range(n_heads):
            qk_head(c, h, dst_ref)

    def batch(n, c, c_next, mask=None, bias=None):
        bufs = (sa_ref, sb_ref)
        if c_next is not None:
            qk(c_next, bufs[(n + 1) % 2])
        for h in range(n_heads):
            softmax_pv_head(c, h, bufs[n % 2], mask, bias)

    def start_stream(c):
        for h in range(lookahead):
            qk_head(c, h, sa_ref)

    def stream(n, c, c_next, mask=None, bias=None):
        del n
        for h in range(n_heads):
            t = h + lookahead
            if t < n_heads:
                qk_head(c, t, sa_ref)
            elif c_next is not None:
                qk_head(c_next, t - n_heads, sa_ref)
            softmax_pv_head(c, h, sa_ref, mask, bias)

    for sub in range(n_sub):
        i = pl.program_id(1) * n_sub + sub
        cols = slice(sub * tq, (sub + 1) * tq)
        for h in range(n_heads):
            g = h // rep
            parts = [zeros_h, zeros_h]
            parts[g] = qT_ref[0, HEAD_DIM * h:HEAD_DIM * (h + 1), cols]
            if selected:
                parts += [selb_ref[0, g, :, cols], zeros_h]
            rhs_ref[h] = jnp.concatenate(parts, axis=0)
            if use_sink:
                m_ref[h] = jnp.full((1, tq), sink_ref[h] * LOG2E, F32)
                acc_ref[h] = jnp.where(den_row, 1.0, 0.0)
            else:
                m_ref[h] = jnp.full((1, tq), NEG_INF, F32)
                acc_ref[h] = jnp.zeros((acc_rows, tq), F32)

        if selected:
            start_stream(0)
            odd = i % 2
            last = i - odd

            def full_chunks(j, carry):
                stream(0, 2 * j, 2 * j + 1)
                stream(0, 2 * j + 1, 2 * j + 2)
                return carry

            lax.fori_loop(0, last // 2, full_chunks, 0)
            stream(0, last, i, mask=ksub <= qlane + odd * tq)

            @pl.when(odd == 1)
            def _():
                stream(0, i, None, mask=ksub <= qlane)
        else:
            chunks = [jnp.maximum(i - d, 0) for d in range(nprev, 0, -1)] + [i]
            run = stream if lookahead else batch
            if lookahead:
                start_stream(chunks[0])
            else:
                qk(chunks[0], sa_ref)
            for n, d in enumerate(range(nprev, -1, -1)):
                nxt = chunks[n + 1] if n + 1 < len(chunks) else None
                if d == 0:
                    run(n, chunks[n], nxt, mask=ksub <= qlane)
                elif d == nprev:
                    gone = jnp.where(i >= d, 0, tq)
                    run(n, chunks[n], nxt, mask=ksub > qlane + gone)
                else:
                    run(n, chunks[n], nxt, bias=jnp.where(i >= d, 0.0, MASK_VAL))

        heads = []
        for h in range(n_heads):
            acc = acc_ref[h]
            o = acc[0:HEAD_DIM] * (1.0 / jnp.maximum(acc[HEAD_DIM:HEAD_DIM + 1], TINY))
            if gate_branch is not None:
                o = o * gT_ref[0, 3 * h + gate_branch:3 * h + gate_branch + 1, cols]
            heads.append(o)
        _store_head_pairs(o_ref, heads, cols)


def _attention(qT, k, vT, *, tq, rep, n_sub=1, nprev=0, selected=False, gnT=None, gate_branch=None,
               selb=None, sinks=None, lookahead=QK_LOOKAHEAD):
    B, F, S = qT.shape
    n_heads = F // HEAD_DIM
    n_kv = vT.shape[2]
    tstep = tq * n_sub
    assert vT.shape == (B, S // tq, n_kv, tq) and S % tstep == 0
    in_specs = [pl.BlockSpec((1, F, tstep), lambda b, i: (b, 0, i)),
                pl.BlockSpec((1, S, LANES), lambda b, i: (b, 0, 0)),
                pl.BlockSpec((1, S // tq, n_kv, tq), lambda b, i: (b, 0, 0, 0))]
    args = [qT, k, vT]
    if gate_branch is not None:
        in_specs.append(pl.BlockSpec((1, GATE_ROWS, tstep), lambda b, i: (b, 0, i)))
        args.append(gnT)
    if selected:
        onehot = (jnp.arange(S)[:, None] // SEL_BLOCK == jnp.arange(LANES)[None, :]).astype(BF16)
        in_specs += [pl.BlockSpec((1, NSA_GROUPS, MAX_SEL_BLOCKS, tstep), lambda b, i: (b, 0, 0, i)),
                     pl.BlockSpec((S, LANES), lambda b, i: (0, 0))]
        args += [selb, onehot]
    if sinks is not None:
        in_specs.append(pl.BlockSpec(memory_space=pltpu.SMEM))
        args.append(sinks)
    kern = functools.partial(_attn_kernel, n_heads=n_heads, rep=rep, n_sub=n_sub, nprev=nprev,
                             selected=selected, use_sink=sinks is not None, gate_branch=gate_branch,
                             lookahead=lookahead)
    contraction = 2 * LANES if selected else LANES
    return pl.pallas_call(
        kern,
        grid=(B, S // tstep),
        in_specs=in_specs,
        out_specs=pl.BlockSpec((1, tstep, F), lambda b, i: (b, i, 0)),
        out_shape=jax.ShapeDtypeStruct((B, S, F), BF16),
        scratch_shapes=[pltpu.VMEM((n_heads, contraction, tq), BF16),
                        pltpu.VMEM((n_heads, HEAD_DIM + DEN_ROWS, tq), F32),
                        pltpu.VMEM((n_heads, 1, tq), F32)]
        + [pltpu.VMEM((n_heads, tq, tq), F32)] * (1 if lookahead else 2),
        compiler_params=pltpu.CompilerParams(dimension_semantics=("parallel", "parallel"),
                                             vmem_limit_bytes=VMEM_LIMIT),
        name="attn_sel" if selected else ("attn_swa" if sinks is not None else "attn_win"),
    )(*args)


def _layer_norm(r, g, b):
    mu = jnp.mean(r, axis=-1, keepdims=True)
    d = r - mu
    var = jnp.mean(d * d, axis=-1, keepdims=True)
    return d * lax.rsqrt(var + LN_EPS) * g + b


def _mlp_kernel(x_ref, oa_ref, oc_ref, os_ref, ow_ref, gm_ref, p_ref, wa_ref, wb_ref, wo_ref, g1_ref, b1_ref,
                wg_ref, wu_ref, wd_ref, wpg_ref, wple_ref, g2_ref, b2_ref, o_ref, *, alpha):
    tm = x_ref.shape[0]
    blocks = [slice(r * tm // MLP_SPLIT, (r + 1) * tm // MLP_SPLIT) for r in range(MLP_SPLIT)]

    def merge(r):
        o_b = (oc_ref[r, :].astype(F32) + os_ref[r, :].astype(F32) + ow_ref[r, :].astype(F32)).astype(BF16)
        return (gm_ref[r, 0:D_MODEL].astype(F32) * _dot(oa_ref[r, :], wa_ref[...])
                + gm_ref[r, D_MODEL:2 * D_MODEL].astype(F32) * _dot(o_b, wb_ref[...]))

    def norm1(r, y):
        return _layer_norm(alpha * x_ref[r, :] + _dot(y.astype(BF16), wo_ref[...]), g1_ref[...], b1_ref[...])

    def swiglu(h):
        hb = h.astype(BF16)
        ff = jnp.zeros(h.shape, F32)
        for lo, hi in FF_CHUNKS:
            sl = slice(lo, hi)
            a = jax.nn.silu(_dot(hb, wg_ref[:, sl])) * _dot(hb, wu_ref[:, sl])
            ff = ff + _dot(a.astype(BF16), wd_ref[sl, :])
        return ff

    def norm2(r, h, ff):
        ple = jax.nn.sigmoid(_dot(h.astype(BF16), wpg_ref[...])) * _dot(p_ref[r, :].astype(BF16), wple_ref[...])
        o_ref[r, :] = _layer_norm(alpha * h + ff + ple, g2_ref[...], b2_ref[...])

    ys = [merge(r) for r in blocks]
    hs = [norm1(r, y) for r, y in zip(blocks, ys)]
    ffs = [swiglu(h) for h in hs]
    for r, h, ff in zip(blocks, hs, ffs):
        norm2(r, h, ff)


def _mlp(x2, o_a, o_c, o_s, o_w, gm, p2, w_a, w_b, w_o, ln1_g, ln1_b, w_g, w_u, w_d, w_pg, w_ple, ln2_g, ln2_b,
         alpha):
    T, D = x2.shape
    tm = MLP_TM
    row = lambda w: pl.BlockSpec((tm, w), lambda i: (i, 0))
    const = lambda shp: pl.BlockSpec(shp, lambda i: (0, 0), pipeline_mode=pl.Buffered(1))
    bf = lambda w: w.astype(BF16)
    return pl.pallas_call(
        functools.partial(_mlp_kernel, alpha=alpha),
        grid=(T // tm,),
        in_specs=[row(D), row(SWA_Q), row(NSA_Q), row(NSA_Q), row(NSA_Q), row(2 * D), row(PLE_DIM),
                  const((SWA_Q, D)), const((NSA_Q, D)), const((D, D)), const((1, D)), const((1, D)),
                  const((D, D_FF)), const((D, D_FF)), const((D_FF, D)), const((D, D)), const((PLE_DIM, D)),
                  const((1, D)), const((1, D))],
        out_specs=row(D),
        out_shape=jax.ShapeDtypeStruct((T, D), F32),
        compiler_params=pltpu.CompilerParams(dimension_semantics=("parallel",),
                                             vmem_limit_bytes=MLP_VMEM_LIMIT),
        name="mlp_ln",
    )(x2, o_a, o_c, o_s, o_w, gm, p2, bf(w_a), bf(w_b), bf(w_o), ln1_g[None, :], ln1_b[None, :],
      bf(w_g), bf(w_u), bf(w_d), bf(w_pg), bf(w_ple), ln2_g[None, :], ln2_b[None, :])


def kernel(x, p, positions, w_in, attn_sinks, cmp_pos_emb, w_cmp_k1, w_cmp_k2, w_cmp_v1, w_cmp_v2, w_proj_swa, w_proj_nsa, w_out, ln1_g, ln1_b, w_ff_gate, w_ff_up, w_ff_down, w_ple, w_ple_gate, ln2_g, ln2_b):
    B, S, D = x.shape
    depth = w_in.shape[0]
    assert D == D_MODEL and S % INPROJ_TM == 0 and S >= NSA_WINDOW + ATTN_TQ
    alpha = (2.0 * depth) ** 0.25
    h = x
    for i in range(depth):
        (ks, kw, ka, kc, vc, gm, qaT, qnrT, qnnT, vsT, vwT, vaT, gnT) = _inproj(h, positions, w_in[i])
        k_c, v_cT = _compress(kc, vc, cmp_pos_emb[i], w_cmp_k1[i], w_cmp_k2[i], w_cmp_v1[i], w_cmp_v2[i])
        o_cmp, selb = _cmp_select(qnnT, k_c, v_cT, gnT)
        o_slc = _attention(qnrT, ks, vsT, tq=SEL_TQ, rep=NSA_REP, n_sub=2, selected=True, gnT=gnT, gate_branch=1,
                           selb=selb)
        o_win = _attention(qnrT, kw, vwT, tq=ATTN_TQ, rep=NSA_REP, n_sub=2, nprev=NSA_WINDOW // ATTN_TQ, gnT=gnT,
                           gate_branch=2)
        o_swa = _attention(qaT, ka, vaT, tq=SWA_TQ, rep=SWA_HEADS, n_sub=SWA_SUBTILES,
                           nprev=SWA_WINDOW // SWA_TQ, sinks=attn_sinks[i].astype(F32), lookahead=0)
        flat = lambda a: a.reshape(B * S, a.shape[-1])
        h2 = _mlp(flat(h), flat(o_swa), flat(o_cmp), flat(o_slc), flat(o_win), flat(gm), flat(p[i]),
                  w_proj_swa[i], w_proj_nsa[i], w_out[i], ln1_g[i], ln1_b[i],
                  w_ff_gate[i], w_ff_up[i], w_ff_down[i], w_ple_gate[i], w_ple[i], ln2_g[i], ln2_b[i], alpha)
        h = h2.reshape(B, S, D)
    return h
```

```python
import functools

import jax
import jax.numpy as jnp
from jax import lax
from jax.experimental import pallas as pl
from jax.experimental.pallas import tpu as pltpu

F32 = jnp.float32
BF16 = jnp.bfloat16

D_MODEL = 1024
HEAD_DIM = 64
HALF_DIM = HEAD_DIM // 2
ROPE_THETA = 10000.0
LN_EPS = 1e-5
NEG_INF = -1e30
FORCE = 1e30
TINY = 1e-30
PLE_DIM = 256

SWA_HEADS = 8
SWA_WINDOW = 128
NSA_HEADS = 8
NSA_GROUPS = 2
NSA_REP = NSA_HEADS // NSA_GROUPS
NSA_WINDOW = 512
CMP_BLOCK = 32
CMP_STRIDE = 16
CMP_HIDDEN = 256
SEL_BLOCK = 64
N_SEL = 16
D_FF = 2816

SWA_Q = SWA_HEADS * HEAD_DIM
NSA_Q = NSA_HEADS * HEAD_DIM
NSA_KV = NSA_GROUPS * HEAD_DIM
NSA_GATES = NSA_HEADS * 3
GATE_ROWS = 32
MAX_SEL_BLOCKS = 64
DEN_ROWS = 16
LOG2E = 1.4426950408889634

LANES = 128
VMEM_LIMIT = 56 * 1024 * 1024
MLP_VMEM_LIMIT = 60 * 1024 * 1024

TOK_KV_COLS = 2 * LANES
TOK_COLS = TOK_KV_COLS + 2 * D_MODEL
FEAT_K_ROWS = 3 * LANES
FEAT_V_ROWS = 2 * NSA_KV + HEAD_DIM + GATE_ROWS + 32
FEAT_ROWS = SWA_Q + NSA_Q + FEAT_K_ROWS + FEAT_V_ROWS

INPROJ_TM = 512
ATTN_TQ = 256
SEL_TQ = 256
SWA_TQ = 128
SWA_SUBTILES = 8
QK_LOOKAHEAD = 7
MLP_TM = 512
MLP_SPLIT = 2
FF_CHUNKS = ((0, 1536), (1536, D_FF))
MASK_VAL = 2.0 * NEG_INF
SUBLANES = 8


def _dot(a, b):
    return jnp.dot(a, b, preferred_element_type=F32)


def _dot_nt(a, b):
    return lax.dot_general(a, b, (((1,), (1,)), ((), ())), preferred_element_type=F32)


def _inproj_kernel(x_ref, posr_ref, invc_ref, wn_ref, wt_ref,
                   ks_ref, kw_ref, ka_ref, kc_ref, vc_ref, gm_ref,
                   qaT_ref, qnrT_ref, qnnT_ref, vsT_ref, vwT_ref, vaT_ref, gnT_ref):
    tm = x_ref.shape[1]
    xb = x_ref[0].astype(BF16)

    zkv = _dot(xb, wn_ref[:, 0:TOK_KV_COLS])
    kc_ref[0] = zkv[:, 0:LANES]
    vc_ref[0] = zkv[:, LANES:2 * LANES]
    gchunk = 512
    for c in range(2 * D_MODEL // gchunk):
        zg = _dot(xb, wn_ref[:, TOK_KV_COLS + c * gchunk:TOK_KV_COLS + (c + 1) * gchunk])
        gm_ref[0, :, c * gchunk:(c + 1) * gchunk] = jax.nn.sigmoid(zg).astype(BF16)

    ang_t = invc_ref[...] * posr_ref[0]
    cos_f = jnp.cos(ang_t)
    sin_f = jnp.sin(ang_t)
    scale = HEAD_DIM ** -0.5 * LOG2E

    def rope_feat(z, n_heads, mul):
        out = []
        for h in range(n_heads):
            x1 = z[HEAD_DIM * h:HEAD_DIM * h + HALF_DIM]
            x2 = z[HEAD_DIM * h + HALF_DIM:HEAD_DIM * (h + 1)]
            out += [(x1 * cos_f - x2 * sin_f) * mul, (x2 * cos_f + x1 * sin_f) * mul]
        return out

    def store_rows(out_ref, pieces):
        for n, piece in enumerate(pieces):
            out_ref[0, HALF_DIM * n:HALF_DIM * (n + 1), :] = piece.astype(BF16)

    zqa = _dot_nt(wt_ref[0:SWA_Q, :], xb)
    store_rows(qaT_ref, rope_feat(zqa, SWA_HEADS, scale))
    zqn = _dot_nt(wt_ref[SWA_Q:SWA_Q + NSA_Q, :], xb)
    store_rows(qnrT_ref, rope_feat(zqn, NSA_HEADS, scale))
    qnnT_ref[0] = (zqn * scale).astype(BF16)
    k0 = SWA_Q + NSA_Q
    zk = _dot_nt(wt_ref[k0:k0 + FEAT_K_ROWS, :], xb)
    kr = rope_feat(zk, FEAT_K_ROWS // HEAD_DIM, 1.0)
    for j, out_ref in enumerate((ks_ref, kw_ref, ka_ref)):
        blk = jnp.concatenate(kr[4 * j:4 * (j + 1)], axis=0)
        out_ref[0] = blk.T.astype(BF16)
    zv = _dot_nt(wt_ref[k0 + FEAT_K_ROWS:FEAT_ROWS, :], xb)
    for j in range(tm // SEL_TQ):
        sl = slice(j * SEL_TQ, (j + 1) * SEL_TQ)
        vsT_ref[0, j] = zv[0:NSA_KV, sl].astype(BF16)
    for j in range(tm // ATTN_TQ):
        sl = slice(j * ATTN_TQ, (j + 1) * ATTN_TQ)
        vwT_ref[0, j] = zv[NSA_KV:2 * NSA_KV, sl].astype(BF16)
    for j in range(tm // SWA_TQ):
        sl = slice(j * SWA_TQ, (j + 1) * SWA_TQ)
        vaT_ref[0, j] = zv[2 * NSA_KV:2 * NSA_KV + HEAD_DIM, sl].astype(BF16)
    g0 = 2 * NSA_KV + HEAD_DIM
    gnT_ref[0] = jax.nn.sigmoid(zv[g0:g0 + GATE_ROWS])


def _inproj(x, positions, w_in):
    B, S, D = x.shape
    tm = INPROJ_TM
    offs = {}
    off = 0
    for name, n in (("qa", SWA_Q), ("ka", HEAD_DIM), ("va", HEAD_DIM), ("qn", NSA_Q), ("kc", NSA_KV),
                    ("vc", NSA_KV), ("ks", NSA_KV), ("vs", NSA_KV), ("kw", NSA_KV), ("vw", NSA_KV),
                    ("gn", NSA_GATES), ("gm", 2 * D_MODEL)):
        offs[name] = (off, off + n)
        off += n
    col = lambda name: w_in[:, offs[name][0]:offs[name][1]]
    zpad = lambda n: jnp.zeros((D, n), w_in.dtype)
    wn = jnp.concatenate([col("kc"), col("vc"), col("gm")], axis=1).astype(BF16)
    wt = jnp.concatenate([col("qa"), col("qn"), col("ks"), col("kw"), col("ka"), zpad(LANES - HEAD_DIM),
                          col("vs"), col("vw"), col("va"), col("gn"),
                          zpad(FEAT_V_ROWS - (2 * NSA_KV + HEAD_DIM + NSA_GATES))], axis=1).T.astype(BF16)
    assert wn.shape == (D, TOK_COLS) and wt.shape == (FEAT_ROWS, D)

    inv_col = (ROPE_THETA ** (-jnp.arange(0, HEAD_DIM, 2, dtype=F32) / HEAD_DIM))[:, None]
    pos_row = positions.astype(F32)[:, None, :]

    tok = lambda w: pl.BlockSpec((1, tm, w), lambda b, i: (b, i, 0))
    feat = lambda r: pl.BlockSpec((1, r, tm), lambda b, i: (b, 0, i))
    const = lambda shp: pl.BlockSpec(shp, lambda b, i: (0,) * len(shp))
    out_shape = (
        jax.ShapeDtypeStruct((B, S, LANES), BF16),
        jax.ShapeDtypeStruct((B, S, LANES), BF16),
        jax.ShapeDtypeStruct((B, S, LANES), BF16),
        jax.ShapeDtypeStruct((B, S, LANES), F32),
        jax.ShapeDtypeStruct((B, S, LANES), F32),
        jax.ShapeDtypeStruct((B, S, 2 * D_MODEL), BF16),
        jax.ShapeDtypeStruct((B, SWA_Q, S), BF16),
        jax.ShapeDtypeStruct((B, NSA_Q, S), BF16),
        jax.ShapeDtypeStruct((B, NSA_Q, S), BF16),
        jax.ShapeDtypeStruct((B, S // SEL_TQ, NSA_KV, SEL_TQ), BF16),
        jax.ShapeDtypeStruct((B, S // ATTN_TQ, NSA_KV, ATTN_TQ), BF16),
        jax.ShapeDtypeStruct((B, S // SWA_TQ, HEAD_DIM, SWA_TQ), BF16),
        jax.ShapeDtypeStruct((B, GATE_ROWS, S), F32),
    )
    out_specs = (
        tok(LANES), tok(LANES), tok(LANES), tok(LANES), tok(LANES), tok(2 * D_MODEL),
        feat(SWA_Q), feat(NSA_Q), feat(NSA_Q),
        pl.BlockSpec((1, tm // SEL_TQ, NSA_KV, SEL_TQ), lambda b, i: (b, i, 0, 0)),
        pl.BlockSpec((1, tm // ATTN_TQ, NSA_KV, ATTN_TQ), lambda b, i: (b, i, 0, 0)),
        pl.BlockSpec((1, tm // SWA_TQ, HEAD_DIM, SWA_TQ), lambda b, i: (b, i, 0, 0)),
        feat(GATE_ROWS),
    )
    return pl.pallas_call(
        _inproj_kernel,
        grid=(B, S // tm),
        in_specs=[tok(D), pl.BlockSpec((1, 1, tm), lambda b, i: (b, 0, i)),
                  const((HALF_DIM, 1)), const((D, TOK_COLS)), const((FEAT_ROWS, D))],
        out_specs=out_specs,
        out_shape=out_shape,
        compiler_params=pltpu.CompilerParams(dimension_semantics=("parallel", "parallel"),
                                             vmem_limit_bytes=VMEM_LIMIT),
        name="inproj",
    )(x, pos_row, inv_col, wn, wt)


def _gelu_tanh(x):
    return 0.5 * x * (1.0 + jnp.tanh(0.7978845608028654 * (x + 0.044715 * (x * x * x))))


def _compress_kernel(kc_ref, vc_ref, pe_ref, wbk_ref, w1k_ref, w2k_ref, wbv_ref, w1v_ref, w2vT_ref,
                     kcmp_ref, vcmpT_ref):
    n_ch = kc_ref.shape[1] // CMP_STRIDE

    def hidden(x_ref, wb_ref, w1_ref, pe_row):
        acc = jnp.zeros((n_ch, 2 * NSA_GROUPS * CMP_HIDDEN), F32)
        for t in range(CMP_STRIDE):
            x_t = x_ref[0, pl.ds(t, n_ch, stride=CMP_STRIDE), :].astype(BF16)
            acc = acc + _dot(x_t, wb_ref[t])
        bias = _dot(jnp.broadcast_to(pe_row, (SUBLANES, pe_row.shape[1])).astype(BF16), w1_ref[...])[0:1]
        out = []
        for g in range(NSA_GROUPS):
            u = acc[:, 2 * CMP_HIDDEN * g:2 * CMP_HIDDEN * g + CMP_HIDDEN]
            v = acc[:, 2 * CMP_HIDDEN * g + CMP_HIDDEN:2 * CMP_HIDDEN * (g + 1)]
            out.append(_gelu_tanh(u + pltpu.roll(v, n_ch - 1, 0) + bias).astype(BF16))
        return out

    hk = hidden(kc_ref, wbk_ref, w1k_ref, pe_ref[0:1, :])
    hv = hidden(vc_ref, wbv_ref, w1v_ref, pe_ref[1:2, :])
    for g in range(NSA_GROUPS):
        kcmp_ref[0, g] = _dot(hk[g], w2k_ref[...]).astype(BF16)
        vcmpT_ref[0, g] = _dot_nt(w2vT_ref[...], hv[g]).astype(BF16)


def _compress(kc, vc, pos_emb, w_k1, w_k2, w_v1, w_v2):
    B, S, _ = kc.shape
    n_ch = S // CMP_STRIDE
    half = CMP_STRIDE * HEAD_DIM

    def per_token(w1):
        blk = jnp.concatenate([w1[:half].reshape(CMP_STRIDE, HEAD_DIM, CMP_HIDDEN),
                               w1[half:].reshape(CMP_STRIDE, HEAD_DIM, CMP_HIDDEN)], axis=2)
        z = jnp.zeros_like(blk)
        return jnp.concatenate([jnp.concatenate([blk, z], axis=2), jnp.concatenate([z, blk], axis=2)],
                               axis=1).astype(BF16)

    pe = pos_emb.reshape(2, 2 * half)
    const = lambda shp: pl.BlockSpec(shp, lambda b: (0,) * len(shp))
    wb_shape = (CMP_STRIDE, NSA_KV, 2 * NSA_GROUPS * CMP_HIDDEN)
    return pl.pallas_call(
        _compress_kernel,
        grid=(B,),
        in_specs=[pl.BlockSpec((1, S, NSA_KV), lambda b: (b, 0, 0)),
                  pl.BlockSpec((1, S, NSA_KV), lambda b: (b, 0, 0)),
                  const((2, 2 * half)),
                  const(wb_shape), const((2 * half, CMP_HIDDEN)), const((CMP_HIDDEN, HEAD_DIM)),
                  const(wb_shape), const((2 * half, CMP_HIDDEN)), const((HEAD_DIM, CMP_HIDDEN))],
        out_specs=(pl.BlockSpec((1, NSA_GROUPS, n_ch, HEAD_DIM), lambda b: (b, 0, 0, 0)),
                   pl.BlockSpec((1, NSA_GROUPS, HEAD_DIM, n_ch), lambda b: (b, 0, 0, 0))),
        out_shape=(jax.ShapeDtypeStruct((B, NSA_GROUPS, n_ch, HEAD_DIM), BF16),
                   jax.ShapeDtypeStruct((B, NSA_GROUPS, HEAD_DIM, n_ch), BF16)),
        compiler_params=pltpu.CompilerParams(dimension_semantics=("parallel",),
                                             vmem_limit_bytes=VMEM_LIMIT),
        name="compress",
    )(kc, vc, pe, per_token(w_k1), w_k1.astype(BF16), w_k2.astype(BF16),
      per_token(w_v1), w_v1.astype(BF16), w_v2.T.astype(BF16))


def _store_head_pairs(o_ref, heads, rows=slice(None)):
    for pr in range(len(heads) // 2):
        pair = jnp.concatenate([heads[2 * pr], heads[2 * pr + 1]], axis=0)
        o_ref[0, rows, LANES * pr:LANES * (pr + 1)] = pair.T.astype(BF16)


def _cmp_select_kernel(qT_ref, kc_ref, vcT_ref, gT_ref, o_ref, selb_ref, s_ref, psum_ref, imp_ref, rank_ref, *,
                       n_blk, n_sel):
    tq = qT_ref.shape[2]
    n_ch = kc_ref.shape[2]
    t = pl.program_id(1) * tq + lax.broadcasted_iota(jnp.int32, (1, tq), 1)
    cidx = lax.broadcasted_iota(jnp.int32, (n_ch, 1), 0)
    cmask = (cidx * CMP_STRIDE + (CMP_BLOCK - 1)) <= t
    blk = lax.broadcasted_iota(jnp.int32, (n_blk, tq), 0)
    cur = t >> 6
    forced = (blk == 0) | (blk == cur) | (blk == cur - 1)
    ratio = SEL_BLOCK // CMP_STRIDE

    def attend(nk):
        for h in range(NSA_HEADS):
            s_ref[h, 0:nk, :] = _dot(kc_ref[0, h // NSA_REP, 0:nk, :],
                                     qT_ref[0, HEAD_DIM * h:HEAD_DIM * (h + 1), :])
        heads = []
        for g in range(NSA_GROUPS):
            psum = jnp.zeros((nk, tq), F32)
            for r in range(NSA_REP):
                h = NSA_REP * g + r
                s = jnp.where(cmask[0:nk], s_ref[h, 0:nk, :], MASK_VAL)
                m = jnp.maximum(jnp.max(s, axis=0, keepdims=True), NEG_INF)
                e = jnp.exp2(s - m)
                den = jnp.sum(e, axis=0, keepdims=True)
                p = e * (1.0 / jnp.maximum(den, TINY))
                o = _dot(vcT_ref[0, g, :, 0:nk], p.astype(BF16))
                heads.append(o * gT_ref[0, 3 * h:3 * h + 1, :])
                psum = psum + p
            for j in range(tq // LANES):
                psum_ref[j, 0:nk, :] = psum[:, LANES * j:LANES * (j + 1)]
                if nk < n_ch:
                    psum_ref[j, nk:n_ch, :] = jnp.zeros((n_ch - nk, LANES), F32)
            rows = [jnp.concatenate([psum_ref[j, pl.ds(k, n_blk, stride=ratio), :] for j in range(tq // LANES)],
                                    axis=1) for k in range(ratio)]
            prev = jnp.where(blk == 0, 0.0, pltpu.roll(rows[ratio - 1], 1, 0))
            imp = prev + rows[0] + rows[1] + rows[2] + rows[3]
            imp_ref[g, 0:n_blk, :] = jnp.where(forced, FORCE, jnp.where(blk <= cur, imp, NEG_INF))
            if n_blk < MAX_SEL_BLOCKS:
                imp_ref[g, n_blk:MAX_SEL_BLOCKS, :] = jnp.full((MAX_SEL_BLOCKS - n_blk, tq), NEG_INF, F32)
            rank_ref[g] = jnp.zeros((MAX_SEL_BLOCKS, tq), jnp.int32)
        _store_head_pairs(o_ref, heads)

    early = pl.program_id(1) < pl.num_programs(1) // 2
    pl.when(early)(lambda: attend(n_ch // 2))
    pl.when(jnp.logical_not(early))(lambda: attend(n_ch))

    last_cur = (pl.program_id(1) * tq + tq - 1) >> 6
    sub = lax.broadcasted_iota(jnp.int32, (SUBLANES, tq), 0)
    n_groups = -(-n_blk // SUBLANES)
    for rg in range(n_groups):
        @pl.when(rg * SUBLANES <= last_cur)
        def _(rg=rg):
            for g in range(NSA_GROUPS):
                src = imp_ref[g, SUBLANES * rg:SUBLANES * (rg + 1), :]
                for v in range(n_groups):
                    tgt = imp_ref[g, SUBLANES * v:SUBLANES * (v + 1), :]
                    cnt = jnp.zeros((SUBLANES, tq), jnp.int32)
                    for r in range(SUBLANES):
                        row = src[r:r + 1, :]
                        if v > rg:
                            cnt = cnt + jnp.where(row >= tgt, 1, 0)
                        elif v < rg:
                            cnt = cnt + jnp.where(row > tgt, 1, 0)
                        else:
                            cnt = cnt + jnp.where(sub > r, jnp.where(row >= tgt, 1, 0), jnp.where(row > tgt, 1, 0))
                    rank_ref[g, SUBLANES * v:SUBLANES * (v + 1), :] += cnt
    for g in range(NSA_GROUPS):
        selb_ref[0, g] = jnp.where(rank_ref[g] < n_sel, 0.0, MASK_VAL).astype(BF16)


def _cmp_select(qnnT, k_c, v_cT, gnT):
    B, _, S = qnnT.shape
    tq = ATTN_TQ
    n_ch = k_c.shape[2]
    n_blk = S // SEL_BLOCK
    assert n_blk <= MAX_SEL_BLOCKS
    kern = functools.partial(_cmp_select_kernel, n_blk=n_blk, n_sel=min(N_SEL, n_blk))
    return pl.pallas_call(
        kern,
        grid=(B, S // tq),
        in_specs=[pl.BlockSpec((1, NSA_Q, tq), lambda b, i: (b, 0, i)),
                  pl.BlockSpec((1, NSA_GROUPS, n_ch, HEAD_DIM), lambda b, i: (b, 0, 0, 0)),
                  pl.BlockSpec((1, NSA_GROUPS, HEAD_DIM, n_ch), lambda b, i: (b, 0, 0, 0)),
                  pl.BlockSpec((1, GATE_ROWS, tq), lambda b, i: (b, 0, i))],
        out_specs=(pl.BlockSpec((1, tq, NSA_Q), lambda b, i: (b, i, 0)),
                   pl.BlockSpec((1, NSA_GROUPS, MAX_SEL_BLOCKS, tq), lambda b, i: (b, 0, 0, i))),
        out_shape=(jax.ShapeDtypeStruct((B, S, NSA_Q), BF16),
                   jax.ShapeDtypeStruct((B, NSA_GROUPS, MAX_SEL_BLOCKS, S), BF16)),
        scratch_shapes=[pltpu.VMEM((NSA_HEADS, n_ch, tq), F32),
                        pltpu.VMEM((tq // LANES, n_ch, LANES), F32),
                        pltpu.VMEM((NSA_GROUPS, MAX_SEL_BLOCKS, tq), F32),
                        pltpu.VMEM((NSA_GROUPS, MAX_SEL_BLOCKS, tq), jnp.int32)],
        compiler_params=pltpu.CompilerParams(dimension_semantics=("parallel", "parallel"),
                                             vmem_limit_bytes=VMEM_LIMIT),
        name="cmp_select",
    )(qnnT, k_c, v_cT, gnT)


def _attn_kernel(*refs, n_heads, rep, n_sub, nprev, selected, use_sink, gate_branch, lookahead):
    refs = list(refs)
    qT_ref, k_ref, vT_ref = refs[:3]
    pos = 3
    gT_ref = selb_ref = e_ref = sink_ref = None
    if gate_branch is not None:
        gT_ref = refs[pos]; pos += 1
    if selected:
        selb_ref, e_ref = refs[pos], refs[pos + 1]; pos += 2
    if use_sink:
        sink_ref = refs[pos]; pos += 1
    o_ref = refs[pos]
    rhs_ref, acc_ref, m_ref, sa_ref = refs[pos + 1:pos + 5]
    sb_ref = None if lookahead else refs[pos + 5]

    tq = sa_ref.shape[1]
    ksub = lax.broadcasted_iota(jnp.int32, (tq, tq), 0)
    qlane = lax.broadcasted_iota(jnp.int32, (tq, tq), 1)
    zeros_h = jnp.zeros((HEAD_DIM, tq), BF16)
    ones_rows = jnp.ones((DEN_ROWS, tq), BF16)
    acc_rows = HEAD_DIM + DEN_ROWS
    den_row = lax.broadcasted_iota(jnp.int32, (acc_rows, tq), 0) >= HEAD_DIM
    groups = sorted({h // rep for h in range(n_heads)})

    def qk_head(c, h, dst_ref):
        start = pl.multiple_of(c * tq, tq)
        k = k_ref[0, pl.ds(start, tq), :]
        if selected:
            k = jnp.concatenate([k, e_ref[pl.ds(start, tq), :]], axis=1)
        dst_ref[h] = _dot(k, rhs_ref[h])

    def softmax_pv_head(c, h, src_ref, mask, bias):
        g = h // rep
        v_aug = jnp.concatenate([vT_ref[0, c, HEAD_DIM * g:HEAD_DIM * (g + 1), :], ones_rows], axis=0)
        s = src_ref[h]
        if bias is not None:
            s = s + bias
        if mask is not None:
            s = jnp.where(mask, s, MASK_VAL)
        m_old = m_ref[h]
        m_new = jnp.maximum(m_old, jnp.max(s, axis=0, keepdims=True))
        p = jnp.exp2((s - m_new).astype(BF16))
        acc_ref[h] = jnp.exp2(m_old - m_new) * acc_ref[h] + _dot(v_aug, p)
        m_ref[h] = m_new

    def qk(c, dst_ref):
        for h in range(n_heads):
            qk_head(c, h, dst_ref)

    def batch(n, c, c_next, mask=None, bias=None):
        bufs = (sa_ref, sb_ref)
        if c_next is not None:
            qk(c_next, bufs[(n + 1) % 2])
        for h in range(n_heads):
            softmax_pv_head(c, h, bufs[n % 2], mask, bias)

    def start_stream(c):
        for h in range(lookahead):
            qk_head(c, h, sa_ref)

    def stream(n, c, c_next, mask=None, bias=None):
        del n
        for h in range(n_heads):
            t = h + lookahead
            if t < n_heads:
                qk_head(c, t, sa_ref)
            elif c_next is not None:
                qk_head(c_next, t - n_heads, sa_ref)
            softmax_pv_head(c, h, sa_ref, mask, bias)

    for sub in range(n_sub):
        i = pl.program_id(1) * n_sub + sub
        cols = slice(sub * tq, (sub + 1) * tq)
        for h in range(n_heads):
            g = h // rep
            parts = [zeros_h, zeros_h]
            parts[g] = qT_ref[0, HEAD_DIM * h:HEAD_DIM * (h + 1), cols]
            if selected:
                parts += [selb_ref[0, g, :, cols], zeros_h]
            rhs_ref[h] = jnp.concatenate(parts, axis=0)
            if use_sink:
                m_ref[h] = jnp.full((1, tq), sink_ref[h] * LOG2E, F32)
                acc_ref[h] = jnp.where(den_row, 1.0, 0.0)
            else:
                m_ref[h] = jnp.full((1, tq), NEG_INF, F32)
                acc_ref[h] = jnp.zeros((acc_rows, tq), F32)

        if selected:
            start_stream(0)
            odd = i % 2
            last = i - odd

            def full_chunks(j, carry):
                stream(0, 2 * j, 2 * j + 1)
                stream(0, 2 * j + 1, 2 * j + 2)
                return carry

            lax.fori_loop(0, last // 2, full_chunks, 0)
            stream(0, last, i, mask=ksub <= qlane + odd * tq)

            @pl.when(odd == 1)
            def _():
                stream(0, i, None, mask=ksub <= qlane)
        else:
            chunks = [jnp.maximum(i - d, 0) for d in range(nprev, 0, -1)] + [i]
            run = stream if lookahead else batch
            if lookahead:
                start_stream(chunks[0])
            else:
                qk(chunks[0], sa_ref)
            for n, d in enumerate(range(nprev, -1, -1)):
                nxt = chunks[n + 1] if n + 1 < len(chunks) else None
                if d == 0:
                    run(n, chunks[n], nxt, mask=ksub <= qlane)
                elif d == nprev:
                    gone = jnp.where(i >= d, 0, tq)
                    run(n, chunks[n], nxt, mask=ksub > qlane + gone)
                else:
                    run(n, chunks[n], nxt, bias=jnp.where(i >= d, 0.0, MASK_VAL))

        heads = []
        for h in range(n_heads):
            acc = acc_ref[h]
            o = acc[0:HEAD_DIM] * (1.0 / jnp.maximum(acc[HEAD_DIM:HEAD_DIM + 1], TINY))
            if gate_branch is not None:
                o = o * gT_ref[0, 3 * h + gate_branch:3 * h + gate_branch + 1, cols]
            heads.append(o)
        _store_head_pairs(o_ref, heads, cols)


def _attention(qT, k, vT, *, tq, rep, n_sub=1, nprev=0, selected=False, gnT=None, gate_branch=None,
               selb=None, sinks=None, lookahead=QK_LOOKAHEAD):
    B, F, S = qT.shape
    n_heads = F // HEAD_DIM
    n_kv = vT.shape[2]
    tstep = tq * n_sub
    assert vT.shape == (B, S // tq, n_kv, tq) and S % tstep == 0
    in_specs = [pl.BlockSpec((1, F, tstep), lambda b, i: (b, 0, i)),
                pl.BlockSpec((1, S, LANES), lambda b, i: (b, 0, 0)),
                pl.BlockSpec((1, S // tq, n_kv, tq), lambda b, i: (b, 0, 0, 0))]
    args = [qT, k, vT]
    if gate_branch is not None:
        in_specs.append(pl.BlockSpec((1, GATE_ROWS, tstep), lambda b, i: (b, 0, i)))
        args.append(gnT)
    if selected:
        onehot = (jnp.arange(S)[:, None] // SEL_BLOCK == jnp.arange(LANES)[None, :]).astype(BF16)
        in_specs += [pl.BlockSpec((1, NSA_GROUPS, MAX_SEL_BLOCKS, tstep), lambda b, i: (b, 0, 0, i)),
                     pl.BlockSpec((S, LANES), lambda b, i: (0, 0))]
        args += [selb, onehot]
    if sinks is not None:
        in_specs.append(pl.BlockSpec(memory_space=pltpu.SMEM))
        args.append(sinks)
    kern = functools.partial(_attn_kernel, n_heads=n_heads, rep=rep, n_sub=n_sub, nprev=nprev,
                             selected=selected, use_sink=sinks is not None, gate_branch=gate_branch,
                             lookahead=lookahead)
    contraction = 2 * LANES if selected else LANES
    return pl.pallas_call(
        kern,
        grid=(B, S // tstep),
        in_specs=in_specs,
        out_specs=pl.BlockSpec((1, tstep, F), lambda b, i: (b, i, 0)),
        out_shape=jax.ShapeDtypeStruct((B, S, F), BF16),
        scratch_shapes=[pltpu.VMEM((n_heads, contraction, tq), BF16),
                        pltpu.VMEM((n_heads, HEAD_DIM + DEN_ROWS, tq), F32),
                        pltpu.VMEM((n_heads, 1, tq), F32)]
        + [pltpu.VMEM((n_heads, tq, tq), F32)] * (1 if lookahead else 2),
        compiler_params=pltpu.CompilerParams(dimension_semantics=("parallel", "parallel"),
                                             vmem_limit_bytes=VMEM_LIMIT),
        name="attn_sel" if selected else ("attn_swa" if sinks is not None else "attn_win"),
    )(*args)


def _layer_norm(r, g, b):
    mu = jnp.mean(r, axis=-1, keepdims=True)
    d = r - mu
    var = jnp.mean(d * d, axis=-1, keepdims=True)
    return d * lax.rsqrt(var + LN_EPS) * g + b


def _mlp_kernel(x_ref, oa_ref, oc_ref, os_ref, ow_ref, gm_ref, p_ref, wa_ref, wb_ref, wo_ref, g1_ref, b1_ref,
                wg_ref, wu_ref, wd_ref, wpg_ref, wple_ref, g2_ref, b2_ref, o_ref, *, alpha):
    tm = x_ref.shape[0]
    blocks = [slice(r * tm // MLP_SPLIT, (r + 1) * tm // MLP_SPLIT) for r in range(MLP_SPLIT)]

    def merge(r):
        o_b = (oc_ref[r, :].astype(F32) + os_ref[r, :].astype(F32) + ow_ref[r, :].astype(F32)).astype(BF16)
        return (gm_ref[r, 0:D_MODEL].astype(F32) * _dot(oa_ref[r, :], wa_ref[...])
                + gm_ref[r, D_MODEL:2 * D_MODEL].astype(F32) * _dot(o_b, wb_ref[...]))

    def norm1(r, y):
        return _layer_norm(alpha * x_ref[r, :] + _dot(y.astype(BF16), wo_ref[...]), g1_ref[...], b1_ref[...])

    def swiglu(h):
        hb = h.astype(BF16)
        ff = jnp.zeros(h.shape, F32)
        for lo, hi in FF_CHUNKS:
            sl = slice(lo, hi)
            a = jax.nn.silu(_dot(hb, wg_ref[:, sl])) * _dot(hb, wu_ref[:, sl])
            ff = ff + _dot(a.astype(BF16), wd_ref[sl, :])
        return ff

    def norm2(r, h, ff):
        ple = jax.nn.sigmoid(_dot(h.astype(BF16), wpg_ref[...])) * _dot(p_ref[r, :].astype(BF16), wple_ref[...])
        o_ref[r, :] = _layer_norm(alpha * h + ff + ple, g2_ref[...], b2_ref[...])

    ys = [merge(r) for r in blocks]
    hs = [norm1(r, y) for r, y in zip(blocks, ys)]
    ffs = [swiglu(h) for h in hs]
    for r, h, ff in zip(blocks, hs, ffs):
        norm2(r, h, ff)


def _mlp(x2, o_a, o_c, o_s, o_w, gm, p2, w_a, w_b, w_o, ln1_g, ln1_b, w_g, w_u, w_d, w_pg, w_ple, ln2_g, ln2_b,
         alpha):
    T, D = x2.shape
    tm = MLP_TM
    row = lambda w: pl.BlockSpec((tm, w), lambda i: (i, 0))
    const = lambda shp: pl.BlockSpec(shp, lambda i: (0, 0), pipeline_mode=pl.Buffered(1))
    bf = lambda w: w.astype(BF16)
    return pl.pallas_call(
        functools.partial(_mlp_kernel, alpha=alpha),
        grid=(T // tm,),
        in_specs=[row(D), row(SWA_Q), row(NSA_Q), row(NSA_Q), row(NSA_Q), row(2 * D), row(PLE_DIM),
                  const((SWA_Q, D)), const((NSA_Q, D)), const((D, D)), const((1, D)), const((1, D)),
                  const((D, D_FF)), const((D, D_FF)), const((D_FF, D)), const((D, D)), const((PLE_DIM, D)),
                  const((1, D)), const((1, D))],
        out_specs=row(D),
        out_shape=jax.ShapeDtypeStruct((T, D), F32),
        compiler_params=pltpu.CompilerParams(dimension_semantics=("parallel",),
                                             vmem_limit_bytes=MLP_VMEM_LIMIT),
        name="mlp_ln",
    )(x2, o_a, o_c, o_s, o_w, gm, p2, bf(w_a), bf(w_b), bf(w_o), ln1_g[None, :], ln1_b[None, :],
      bf(w_g), bf(w_u), bf(w_d), bf(w_pg), bf(w_ple), ln2_g[None, :], ln2_b[None, :])


def kernel(x, p, positions, w_in, attn_sinks, cmp_pos_emb, w_cmp_k1, w_cmp_k2, w_cmp_v1, w_cmp_v2, w_proj_swa, w_proj_nsa, w_out, ln1_g, ln1_b, w_ff_gate, w_ff_up, w_ff_down, w_ple, w_ple_gate, ln2_g, ln2_b):
    B, S, D = x.shape
    depth = w_in.shape[0]
    assert D == D_MODEL and S % INPROJ_TM == 0 and S >= NSA_WINDOW + ATTN_TQ
    alpha = (2.0 * depth) ** 0.25
    h = x
    for i in range(depth):
        (ks, kw, ka, kc, vc, gm, qaT, qnrT, qnnT, vsT, vwT, vaT, gnT) = _inproj(h, positions, w_in[i])
        k_c, v_cT = _compress(kc, vc, cmp_pos_emb[i], w_cmp_k1[i], w_cmp_k2[i], w_cmp_v1[i], w_cmp_v2[i])
        o_cmp, selb = _cmp_select(qnnT, k_c, v_cT, gnT)
        o_slc = _attention(qnrT, ks, vsT, tq=SEL_TQ, rep=NSA_REP, n_sub=2, selected=True, gnT=gnT, gate_branch=1,
                           selb=selb)
        o_win = _attention(qnrT, kw, vwT, tq=ATTN_TQ, rep=NSA_REP, n_sub=4, nprev=NSA_WINDOW // ATTN_TQ, gnT=gnT,
                           gate_branch=2)
        o_swa = _attention(qaT, ka, vaT, tq=SWA_TQ, rep=SWA_HEADS, n_sub=SWA_SUBTILES,
                           nprev=SWA_WINDOW // SWA_TQ, sinks=attn_sinks[i].astype(F32), lookahead=0)
        flat = lambda a: a.reshape(B * S, a.shape[-1])
        h2 = _mlp(flat(h), flat(o_swa), flat(o_cmp), flat(o_slc), flat(o_win), flat(gm), flat(p[i]),
                  w_proj_swa[i], w_proj_nsa[i], w_out[i], ln1_g[i], ln1_b[i],
                  w_ff_gate[i], w_ff_up[i], w_ff_down[i], w_ple_gate[i], w_ple[i], ln2_g[i], ln2_b[i], alpha)
        h = h2.reshape(B, S, D)
    return h
```

```python
import functools

import jax
import jax.numpy as jnp
from jax import lax
from jax.experimental import pallas as pl
from jax.experimental.pallas import tpu as pltpu

F32 = jnp.float32
BF16 = jnp.bfloat16

D_MODEL = 1024
HEAD_DIM = 64
HALF_DIM = HEAD_DIM // 2
ROPE_THETA = 10000.0
LN_EPS = 1e-5
NEG_INF = -1e30
FORCE = 1e30
TINY = 1e-30
PLE_DIM = 256

SWA_HEADS = 8
SWA_WINDOW = 128
NSA_HEADS = 8
NSA_GROUPS = 2
NSA_REP = NSA_HEADS // NSA_GROUPS
NSA_WINDOW = 512
CMP_BLOCK = 32
CMP_STRIDE = 16
CMP_HIDDEN = 256
SEL_BLOCK = 64
N_SEL = 16
D_FF = 2816

SWA_Q = SWA_HEADS * HEAD_DIM
NSA_Q = NSA_HEADS * HEAD_DIM
NSA_KV = NSA_GROUPS * HEAD_DIM
NSA_GATES = NSA_HEADS * 3
GATE_ROWS = 32
MAX_SEL_BLOCKS = 64
DEN_ROWS = 16
LOG2E = 1.4426950408889634

LANES = 128
VMEM_LIMIT = 56 * 1024 * 1024
MLP_VMEM_LIMIT = 60 * 1024 * 1024

TOK_KV_COLS = 2 * LANES
TOK_COLS = TOK_KV_COLS + 2 * D_MODEL
FEAT_K_ROWS = 3 * LANES
FEAT_V_ROWS = 2 * NSA_KV + HEAD_DIM + GATE_ROWS + 32
FEAT_ROWS = SWA_Q + NSA_Q + FEAT_K_ROWS + FEAT_V_ROWS

INPROJ_TM = 512
ATTN_TQ = 256
SEL_TQ = 256
SWA_TQ = 128
SWA_SUBTILES = 8
QK_LOOKAHEAD = 7
MLP_TM = 512
MLP_SPLIT = 2
FF_CHUNKS = ((0, 1536), (1536, D_FF))
MASK_VAL = 2.0 * NEG_INF
SUBLANES = 8


def _dot(a, b):
    return jnp.dot(a, b, preferred_element_type=F32)


def _dot_nt(a, b):
    return lax.dot_general(a, b, (((1,), (1,)), ((), ())), preferred_element_type=F32)


def _inproj_kernel(x_ref, posr_ref, invc_ref, wn_ref, wt_ref,
                   ks_ref, kw_ref, ka_ref, kc_ref, vc_ref, gm_ref,
                   qaT_ref, qnrT_ref, qnnT_ref, vsT_ref, vwT_ref, vaT_ref, gnT_ref):
    tm = x_ref.shape[1]
    xb = x_ref[0].astype(BF16)

    zkv = _dot(xb, wn_ref[:, 0:TOK_KV_COLS])
    kc_ref[0] = zkv[:, 0:LANES]
    vc_ref[0] = zkv[:, LANES:2 * LANES]
    gchunk = 512
    for c in range(2 * D_MODEL // gchunk):
        zg = _dot(xb, wn_ref[:, TOK_KV_COLS + c * gchunk:TOK_KV_COLS + (c + 1) * gchunk])
        gm_ref[0, :, c * gchunk:(c + 1) * gchunk] = jax.nn.sigmoid(zg).astype(BF16)

    ang_t = invc_ref[...] * posr_ref[0]
    cos_f = jnp.cos(ang_t)
    sin_f = jnp.sin(ang_t)
    scale = HEAD_DIM ** -0.5 * LOG2E

    def rope_feat(z, n_heads, mul):
        out = []
        for h in range(n_heads):
            x1 = z[HEAD_DIM * h:HEAD_DIM * h + HALF_DIM]
            x2 = z[HEAD_DIM * h + HALF_DIM:HEAD_DIM * (h + 1)]
            out += [(x1 * cos_f - x2 * sin_f) * mul, (x2 * cos_f + x1 * sin_f) * mul]
        return out

    def store_rows(out_ref, pieces):
        for n, piece in enumerate(pieces):
            out_ref[0, HALF_DIM * n:HALF_DIM * (n + 1), :] = piece.astype(BF16)

    zqa = _dot_nt(wt_ref[0:SWA_Q, :], xb)
    store_rows(qaT_ref, rope_feat(zqa, SWA_HEADS, scale))
    zqn = _dot_nt(wt_ref[SWA_Q:SWA_Q + NSA_Q, :], xb)
    store_rows(qnrT_ref, rope_feat(zqn, NSA_HEADS, scale))
    qnnT_ref[0] = (zqn * scale).astype(BF16)
    k0 = SWA_Q + NSA_Q
    zk = _dot_nt(wt_ref[k0:k0 + FEAT_K_ROWS, :], xb)
    kr = rope_feat(zk, FEAT_K_ROWS // HEAD_DIM, 1.0)
    for j, out_ref in enumerate((ks_ref, kw_ref, ka_ref)):
        blk = jnp.concatenate(kr[4 * j:4 * (j + 1)], axis=0)
        out_ref[0] = blk.T.astype(BF16)
    zv = _dot_nt(wt_ref[k0 + FEAT_K_ROWS:FEAT_ROWS, :], xb)
    for j in range(tm // SEL_TQ):
        sl = slice(j * SEL_TQ, (j + 1) * SEL_TQ)
        vsT_ref[0, j] = zv[0:NSA_KV, sl].astype(BF16)
    for j in range(tm // ATTN_TQ):
        sl = slice(j * ATTN_TQ, (j + 1) * ATTN_TQ)
        vwT_ref[0, j] = zv[NSA_KV:2 * NSA_KV, sl].astype(BF16)
    for j in range(tm // SWA_TQ):
        sl = slice(j * SWA_TQ, (j + 1) * SWA_TQ)
        vaT_ref[0, j] = zv[2 * NSA_KV:2 * NSA_KV + HEAD_DIM, sl].astype(BF16)
    g0 = 2 * NSA_KV + HEAD_DIM
    gnT_ref[0] = jax.nn.sigmoid(zv[g0:g0 + GATE_ROWS])


def _inproj(x, positions, w_in):
    B, S, D = x.shape
    tm = INPROJ_TM
    offs = {}
    off = 0
    for name, n in (("qa", SWA_Q), ("ka", HEAD_DIM), ("va", HEAD_DIM), ("qn", NSA_Q), ("kc", NSA_KV),
                    ("vc", NSA_KV), ("ks", NSA_KV), ("vs", NSA_KV), ("kw", NSA_KV), ("vw", NSA_KV),
                    ("gn", NSA_GATES), ("gm", 2 * D_MODEL)):
        offs[name] = (off, off + n)
        off += n
    col = lambda name: w_in[:, offs[name][0]:offs[name][1]]
    zpad = lambda n: jnp.zeros((D, n), w_in.dtype)
    wn = jnp.concatenate([col("kc"), col("vc"), col("gm")], axis=1).astype(BF16)
    wt = jnp.concatenate([col("qa"), col("qn"), col("ks"), col("kw"), col("ka"), zpad(LANES - HEAD_DIM),
                          col("vs"), col("vw"), col("va"), col("gn"),
                          zpad(FEAT_V_ROWS - (2 * NSA_KV + HEAD_DIM + NSA_GATES))], axis=1).T.astype(BF16)
    assert wn.shape == (D, TOK_COLS) and wt.shape == (FEAT_ROWS, D)

    inv_col = (ROPE_THETA ** (-jnp.arange(0, HEAD_DIM, 2, dtype=F32) / HEAD_DIM))[:, None]
    pos_row = positions.astype(F32)[:, None, :]

    tok = lambda w: pl.BlockSpec((1, tm, w), lambda b, i: (b, i, 0))
    feat = lambda r: pl.BlockSpec((1, r, tm), lambda b, i: (b, 0, i))
    const = lambda shp: pl.BlockSpec(shp, lambda b, i: (0,) * len(shp))
    out_shape = (
        jax.ShapeDtypeStruct((B, S, LANES), BF16),
        jax.ShapeDtypeStruct((B, S, LANES), BF16),
        jax.ShapeDtypeStruct((B, S, LANES), BF16),
        jax.ShapeDtypeStruct((B, S, LANES), F32),
        jax.ShapeDtypeStruct((B, S, LANES), F32),
        jax.ShapeDtypeStruct((B, S, 2 * D_MODEL), BF16),
        jax.ShapeDtypeStruct((B, SWA_Q, S), BF16),
        jax.ShapeDtypeStruct((B, NSA_Q, S), BF16),
        jax.ShapeDtypeStruct((B, NSA_Q, S), BF16),
        jax.ShapeDtypeStruct((B, S // SEL_TQ, NSA_KV, SEL_TQ), BF16),
        jax.ShapeDtypeStruct((B, S // ATTN_TQ, NSA_KV, ATTN_TQ), BF16),
        jax.ShapeDtypeStruct((B, S // SWA_TQ, HEAD_DIM, SWA_TQ), BF16),
        jax.ShapeDtypeStruct((B, GATE_ROWS, S), F32),
    )
    out_specs = (
        tok(LANES), tok(LANES), tok(LANES), tok(LANES), tok(LANES), tok(2 * D_MODEL),
        feat(SWA_Q), feat(NSA_Q), feat(NSA_Q),
        pl.BlockSpec((1, tm // SEL_TQ, NSA_KV, SEL_TQ), lambda b, i: (b, i, 0, 0)),
        pl.BlockSpec((1, tm // ATTN_TQ, NSA_KV, ATTN_TQ), lambda b, i: (b, i, 0, 0)),
        pl.BlockSpec((1, tm // SWA_TQ, HEAD_DIM, SWA_TQ), lambda b, i: (b, i, 0, 0)),
        feat(GATE_ROWS),
    )
    return pl.pallas_call(
        _inproj_kernel,
        grid=(B, S // tm),
        in_specs=[tok(D), pl.BlockSpec((1, 1, tm), lambda b, i: (b, 0, i)),
                  const((HALF_DIM, 1)), const((D, TOK_COLS)), const((FEAT_ROWS, D))],
        out_specs=out_specs,
        out_shape=out_shape,
        compiler_params=pltpu.CompilerParams(dimension_semantics=("parallel", "parallel"),
                                             vmem_limit_bytes=VMEM_LIMIT),
        name="inproj",
    )(x, pos_row, inv_col, wn, wt)


def _gelu_tanh(x):
    return 0.5 * x * (1.0 + jnp.tanh(0.7978845608028654 * (x + 0.044715 * (x * x * x))))


def _compress_kernel(kc_ref, vc_ref, pe_ref, wbk_ref, w1k_ref, w2k_ref, wbv_ref, w1v_ref, w2vT_ref,
                     kcmp_ref, vcmpT_ref):
    n_ch = kc_ref.shape[1] // CMP_STRIDE

    def hidden(x_ref, wb_ref, w1_ref, pe_row):
        acc = jnp.zeros((n_ch, 2 * NSA_GROUPS * CMP_HIDDEN), F32)
        for t in range(CMP_STRIDE):
            x_t = x_ref[0, pl.ds(t, n_ch, stride=CMP_STRIDE), :].astype(BF16)
            acc = acc + _dot(x_t, wb_ref[t])
        bias = _dot(jnp.broadcast_to(pe_row, (SUBLANES, pe_row.shape[1])).astype(BF16), w1_ref[...])[0:1]
        out = []
        for g in range(NSA_GROUPS):
            u = acc[:, 2 * CMP_HIDDEN * g:2 * CMP_HIDDEN * g + CMP_HIDDEN]
            v = acc[:, 2 * CMP_HIDDEN * g + CMP_HIDDEN:2 * CMP_HIDDEN * (g + 1)]
            out.append(_gelu_tanh(u + pltpu.roll(v, n_ch - 1, 0) + bias).astype(BF16))
        return out

    hk = hidden(kc_ref, wbk_ref, w1k_ref, pe_ref[0:1, :])
    hv = hidden(vc_ref, wbv_ref, w1v_ref, pe_ref[1:2, :])
    for g in range(NSA_GROUPS):
        kcmp_ref[0, g] = _dot(hk[g], w2k_ref[...]).astype(BF16)
        vcmpT_ref[0, g] = _dot_nt(w2vT_ref[...], hv[g]).astype(BF16)


def _compress(kc, vc, pos_emb, w_k1, w_k2, w_v1, w_v2):
    B, S, _ = kc.shape
    n_ch = S // CMP_STRIDE
    half = CMP_STRIDE * HEAD_DIM

    def per_token(w1):
        blk = jnp.concatenate([w1[:half].reshape(CMP_STRIDE, HEAD_DIM, CMP_HIDDEN),
                               w1[half:].reshape(CMP_STRIDE, HEAD_DIM, CMP_HIDDEN)], axis=2)
        z = jnp.zeros_like(blk)
        return jnp.concatenate([jnp.concatenate([blk, z], axis=2), jnp.concatenate([z, blk], axis=2)],
                               axis=1).astype(BF16)

    pe = pos_emb.reshape(2, 2 * half)
    const = lambda shp: pl.BlockSpec(shp, lambda b: (0,) * len(shp))
    wb_shape = (CMP_STRIDE, NSA_KV, 2 * NSA_GROUPS * CMP_HIDDEN)
    return pl.pallas_call(
        _compress_kernel,
        grid=(B,),
        in_specs=[pl.BlockSpec((1, S, NSA_KV), lambda b: (b, 0, 0)),
                  pl.BlockSpec((1, S, NSA_KV), lambda b: (b, 0, 0)),
                  const((2, 2 * half)),
                  const(wb_shape), const((2 * half, CMP_HIDDEN)), const((CMP_HIDDEN, HEAD_DIM)),
                  const(wb_shape), const((2 * half, CMP_HIDDEN)), const((HEAD_DIM, CMP_HIDDEN))],
        out_specs=(pl.BlockSpec((1, NSA_GROUPS, n_ch, HEAD_DIM), lambda b: (b, 0, 0, 0)),
                   pl.BlockSpec((1, NSA_GROUPS, HEAD_DIM, n_ch), lambda b: (b, 0, 0, 0))),
        out_shape=(jax.ShapeDtypeStruct((B, NSA_GROUPS, n_ch, HEAD_DIM), BF16),
                   jax.ShapeDtypeStruct((B, NSA_GROUPS, HEAD_DIM, n_ch), BF16)),
        compiler_params=pltpu.CompilerParams(dimension_semantics=("parallel",),
                                             vmem_limit_bytes=VMEM_LIMIT),
        name="compress",
    )(kc, vc, pe, per_token(w_k1), w_k1.astype(BF16), w_k2.astype(BF16),
      per_token(w_v1), w_v1.astype(BF16), w_v2.T.astype(BF16))


def _store_head_pairs(o_ref, heads, rows=slice(None)):
    for pr in range(len(heads) // 2):
        pair = jnp.concatenate([heads[2 * pr], heads[2 * pr + 1]], axis=0)
        o_ref[0, rows, LANES * pr:LANES * (pr + 1)] = pair.T.astype(BF16)


def _cmp_select_kernel(qT_ref, kc_ref, vcT_ref, gT_ref, o_ref, selb_ref, s_ref, psum_ref, imp_ref, rank_ref, *,
                       n_blk, n_sel):
    tq = qT_ref.shape[2]
    n_ch = kc_ref.shape[2]
    t = pl.program_id(1) * tq + lax.broadcasted_iota(jnp.int32, (1, tq), 1)
    cidx = lax.broadcasted_iota(jnp.int32, (n_ch, 1), 0)
    cmask = (cidx * CMP_STRIDE + (CMP_BLOCK - 1)) <= t
    blk = lax.broadcasted_iota(jnp.int32, (n_blk, tq), 0)
    cur = t >> 6
    forced = (blk == 0) | (blk == cur) | (blk == cur - 1)
    ratio = SEL_BLOCK // CMP_STRIDE

    def attend(nk):
        for h in range(NSA_HEADS):
            s_ref[h, 0:nk, :] = _dot(kc_ref[0, h // NSA_REP, 0:nk, :],
                                     qT_ref[0, HEAD_DIM * h:HEAD_DIM * (h + 1), :])
        heads = []
        for g in range(NSA_GROUPS):
            psum = jnp.zeros((nk, tq), F32)
            for r in range(NSA_REP):
                h = NSA_REP * g + r
                s = jnp.where(cmask[0:nk], s_ref[h, 0:nk, :], MASK_VAL)
                m = jnp.maximum(jnp.max(s, axis=0, keepdims=True), NEG_INF)
                e = jnp.exp2(s - m)
                den = jnp.sum(e, axis=0, keepdims=True)
                p = e * (1.0 / jnp.maximum(den, TINY))
                o = _dot(vcT_ref[0, g, :, 0:nk], p.astype(BF16))
                heads.append(o * gT_ref[0, 3 * h:3 * h + 1, :])
                psum = psum + p
            for j in range(tq // LANES):
                psum_ref[j, 0:nk, :] = psum[:, LANES * j:LANES * (j + 1)]
                if nk < n_ch:
                    psum_ref[j, nk:n_ch, :] = jnp.zeros((n_ch - nk, LANES), F32)
            rows = [jnp.concatenate([psum_ref[j, pl.ds(k, n_blk, stride=ratio), :] for j in range(tq // LANES)],
                                    axis=1) for k in range(ratio)]
            prev = jnp.where(blk == 0, 0.0, pltpu.roll(rows[ratio - 1], 1, 0))
            imp = prev + rows[0] + rows[1] + rows[2] + rows[3]
            imp_ref[g, 0:n_blk, :] = jnp.where(forced, FORCE, jnp.where(blk <= cur, imp, NEG_INF))
            if n_blk < MAX_SEL_BLOCKS:
                imp_ref[g, n_blk:MAX_SEL_BLOCKS, :] = jnp.full((MAX_SEL_BLOCKS - n_blk, tq), NEG_INF, F32)
            rank_ref[g] = jnp.zeros((MAX_SEL_BLOCKS, tq), jnp.int32)
        _store_head_pairs(o_ref, heads)

    early = pl.program_id(1) < pl.num_programs(1) // 2
    pl.when(early)(lambda: attend(n_ch // 2))
    pl.when(jnp.logical_not(early))(lambda: attend(n_ch))

    last_cur = (pl.program_id(1) * tq + tq - 1) >> 6
    sub = lax.broadcasted_iota(jnp.int32, (SUBLANES, tq), 0)
    n_groups = -(-n_blk // SUBLANES)
    for rg in range(n_groups):
        @pl.when(rg * SUBLANES <= last_cur)
        def _(rg=rg):
            for g in range(NSA_GROUPS):
                src = imp_ref[g, SUBLANES * rg:SUBLANES * (rg + 1), :]
                for v in range(n_groups):
                    tgt = imp_ref[g, SUBLANES * v:SUBLANES * (v + 1), :]
                    cnt = jnp.zeros((SUBLANES, tq), jnp.int32)
                    for r in range(SUBLANES):
                        row = src[r:r + 1, :]
                        if v > rg:
                            cnt = cnt + jnp.where(row >= tgt, 1, 0)
                        elif v < rg:
                            cnt = cnt + jnp.where(row > tgt, 1, 0)
                        else:
                            cnt = cnt + jnp.where(sub > r, jnp.where(row >= tgt, 1, 0), jnp.where(row > tgt, 1, 0))
                    rank_ref[g, SUBLANES * v:SUBLANES * (v + 1), :] += cnt
    for g in range(NSA_GROUPS):
        selb_ref[0, g] = jnp.where(rank_ref[g] < n_sel, 0.0, MASK_VAL).astype(BF16)


def _cmp_select(qnnT, k_c, v_cT, gnT):
    B, _, S = qnnT.shape
    tq = ATTN_TQ
    n_ch = k_c.shape[2]
    n_blk = S // SEL_BLOCK
    assert n_blk <= MAX_SEL_BLOCKS
    kern = functools.partial(_cmp_select_kernel, n_blk=n_blk, n_sel=min(N_SEL, n_blk))
    return pl.pallas_call(
        kern,
        grid=(B, S // tq),
        in_specs=[pl.BlockSpec((1, NSA_Q, tq), lambda b, i: (b, 0, i)),
                  pl.BlockSpec((1, NSA_GROUPS, n_ch, HEAD_DIM), lambda b, i: (b, 0, 0, 0)),
                  pl.BlockSpec((1, NSA_GROUPS, HEAD_DIM, n_ch), lambda b, i: (b, 0, 0, 0)),
                  pl.BlockSpec((1, GATE_ROWS, tq), lambda b, i: (b, 0, i))],
        out_specs=(pl.BlockSpec((1, tq, NSA_Q), lambda b, i: (b, i, 0)),
                   pl.BlockSpec((1, NSA_GROUPS, MAX_SEL_BLOCKS, tq), lambda b, i: (b, 0, 0, i))),
        out_shape=(jax.ShapeDtypeStruct((B, S, NSA_Q), BF16),
                   jax.ShapeDtypeStruct((B, NSA_GROUPS, MAX_SEL_BLOCKS, S), BF16)),
        scratch_shapes=[pltpu.VMEM((NSA_HEADS, n_ch, tq), F32),
                        pltpu.VMEM((tq // LANES, n_ch, LANES), F32),
                        pltpu.VMEM((NSA_GROUPS, MAX_SEL_BLOCKS, tq), F32),
                        pltpu.VMEM((NSA_GROUPS, MAX_SEL_BLOCKS, tq), jnp.int32)],
        compiler_params=pltpu.CompilerParams(dimension_semantics=("parallel", "parallel"),
                                             vmem_limit_bytes=VMEM_LIMIT),
        name="cmp_select",
    )(qnnT, k_c, v_cT, gnT)


def _attn_kernel(*refs, n_heads, rep, n_sub, nprev, selected, use_sink, gate_branch, lookahead):
    refs = list(refs)
    qT_ref, k_ref, vT_ref = refs[:3]
    pos = 3
    gT_ref = selb_ref = e_ref = sink_ref = None
    if gate_branch is not None:
        gT_ref = refs[pos]; pos += 1
    if selected:
        selb_ref, e_ref = refs[pos], refs[pos + 1]; pos += 2
    if use_sink:
        sink_ref = refs[pos]; pos += 1
    o_ref = refs[pos]
    rhs_ref, acc_ref, m_ref, sa_ref = refs[pos + 1:pos + 5]
    sb_ref = None if lookahead else refs[pos + 5]

    tq = sa_ref.shape[1]
    ksub = lax.broadcasted_iota(jnp.int32, (tq, tq), 0)
    qlane = lax.broadcasted_iota(jnp.int32, (tq, tq), 1)
    zeros_h = jnp.zeros((HEAD_DIM, tq), BF16)
    ones_rows = jnp.ones((DEN_ROWS, tq), BF16)
    acc_rows = HEAD_DIM + DEN_ROWS
    den_row = lax.broadcasted_iota(jnp.int32, (acc_rows, tq), 0) >= HEAD_DIM
    groups = sorted({h // rep for h in range(n_heads)})

    def qk_head(c, h, dst_ref):
        start = pl.multiple_of(c * tq, tq)
        k = k_ref[0, pl.ds(start, tq), :]
        if selected:
            k = jnp.concatenate([k, e_ref[pl.ds(start, tq), :]], axis=1)
        dst_ref[h] = _dot(k, rhs_ref[h])

    def softmax_pv_head(c, h, src_ref, mask, bias):
        g = h // rep
        v_aug = jnp.concatenate([vT_ref[0, c, HEAD_DIM * g:HEAD_DIM * (g + 1), :], ones_rows], axis=0)
        s = src_ref[h]
        if bias is not None:
            s = s + bias
        if mask is not None:
            s = jnp.where(mask, s, MASK_VAL)
        m_old = m_ref[h]
        m_new = jnp.maximum(m_old, jnp.max(s, axis=0, keepdims=True))
        p = jnp.exp2((s - m_new).astype(BF16))
        acc_ref[h] = jnp.exp2(m_old - m_new) * acc_ref[h] + _dot(v_aug, p)
        m_ref[h] = m_new

    def qk(c, dst_ref):
        for h in range(n_heads):
            qk_head(c, h, dst_ref)

    def batch(n, c, c_next, mask=None, bias=None):
        bufs = (sa_ref, sb_ref)
        if c_next is not None:
            qk(c_next, bufs[(n + 1) % 2])
        for h in range(n_heads):
            softmax_pv_head(c, h, bufs[n % 2], mask, bias)

    def start_stream(c):
        for h in range(lookahead):
            qk_head(c, h, sa_ref)

    def stream(n, c, c_next, mask=None, bias=None):
        del n
        for h in range(n_heads):
            t = h + lookahead
            if t < n_heads:
                qk_head(c, t, sa_ref)
            elif c_next is not None:
                qk_head(c_next, t - n_heads, sa_ref)
            softmax_pv_head(c, h, sa_ref, mask, bias)

    for sub in range(n_sub):
        i = pl.program_id(1) * n_sub + sub
        cols = slice(sub * tq, (sub + 1) * tq)
        for h in range(n_heads):
            g = h // rep
            parts = [zeros_h, zeros_h]
            parts[g] = qT_ref[0, HEAD_DIM * h:HEAD_DIM * (h + 1), cols]
            if selected:
                parts += [selb_ref[0, g, :, cols], zeros_h]
            rhs_ref[h] = jnp.concatenate(parts, axis=0)
            if use_sink:
                m_ref[h] = jnp.full((1, tq), sink_ref[h] * LOG2E, F32)
                acc_ref[h] = jnp.where(den_row, 1.0, 0.0)
            else:
                m_ref[h] = jnp.full((1, tq), NEG_INF, F32)
                acc_ref[h] = jnp.zeros((acc_rows, tq), F32)

        if selected:
            start_stream(0)
            odd = i % 2
            last = i - odd

            def full_chunks(j, carry):
                stream(0, 2 * j, 2 * j + 1)
                stream(0, 2 * j + 1, 2 * j + 2)
                return carry

            lax.fori_loop(0, last // 2, full_chunks, 0)
            stream(0, last, i, mask=ksub <= qlane + odd * tq)

            @pl.when(odd == 1)
            def _():
                stream(0, i, None, mask=ksub <= qlane)
        else:
            chunks = [jnp.maximum(i - d, 0) for d in range(nprev, 0, -1)] + [i]
            run = stream if lookahead else batch
            if lookahead:
                start_stream(chunks[0])
            else:
                qk(chunks[0], sa_ref)
            for n, d in enumerate(range(nprev, -1, -1)):
                nxt = chunks[n + 1] if n + 1 < len(chunks) else None
                if d == 0:
                    run(n, chunks[n], nxt, mask=ksub <= qlane)
                elif d == nprev:
                    gone = jnp.where(i >= d, 0, tq)
                    run(n, chunks[n], nxt, mask=ksub > qlane + gone)
                else:
                    run(n, chunks[n], nxt, bias=jnp.where(i >= d, 0.0, MASK_VAL))

        heads = []
        for h in range(n_heads):
            acc = acc_ref[h]
            o = acc[0:HEAD_DIM] * (1.0 / jnp.maximum(acc[HEAD_DIM:HEAD_DIM + 1], TINY))
            if gate_branch is not None:
                o = o * gT_ref[0, 3 * h + gate_branch:3 * h + gate_branch + 1, cols]
            heads.append(o)
        _store_head_pairs(o_ref, heads, cols)


def _attention(qT, k, vT, *, tq, rep, n_sub=1, nprev=0, selected=False, gnT=None, gate_branch=None,
               selb=None, sinks=None, lookahead=QK_LOOKAHEAD):
    B, F, S = qT.shape
    n_heads = F // HEAD_DIM
    n_kv = vT.shape[2]
    tstep = tq * n_sub
    assert vT.shape == (B, S // tq, n_kv, tq) and S % tstep == 0
    in_specs = [pl.BlockSpec((1, F, tstep), lambda b, i: (b, 0, i)),
                pl.BlockSpec((1, S, LANES), lambda b, i: (b, 0, 0)),
                pl.BlockSpec((1, S // tq, n_kv, tq), lambda b, i: (b, 0, 0, 0))]
    args = [qT, k, vT]
    if gate_branch is not None:
        in_specs.append(pl.BlockSpec((1, GATE_ROWS, tstep), lambda b, i: (b, 0, i)))
        args.append(gnT)
    if selected:
        onehot = (jnp.arange(S)[:, None] // SEL_BLOCK == jnp.arange(LANES)[None, :]).astype(BF16)
        in_specs += [pl.BlockSpec((1, NSA_GROUPS, MAX_SEL_BLOCKS, tstep), lambda b, i: (b, 0, 0, i)),
                     pl.BlockSpec((S, LANES), lambda b, i: (0, 0))]
        args += [selb, onehot]
    if sinks is not None:
        in_specs.append(pl.BlockSpec(memory_space=pltpu.SMEM))
        args.append(sinks)
    kern = functools.partial(_attn_kernel, n_heads=n_heads, rep=rep, n_sub=n_sub, nprev=nprev,
                             selected=selected, use_sink=sinks is not None, gate_branch=gate_branch,
                             lookahead=lookahead)
    contraction = 2 * LANES if selected else LANES
    return pl.pallas_call(
        kern,
        grid=(B, S // tstep),
        in_specs=in_specs,
        out_specs=pl.BlockSpec((1, tstep, F), lambda b, i: (b, i, 0)),
        out_shape=jax.ShapeDtypeStruct((B, S, F), BF16),
        scratch_shapes=[pltpu.VMEM((n_heads, contraction, tq), BF16),
                        pltpu.VMEM((n_heads, HEAD_DIM + DEN_ROWS, tq), F32),
                        pltpu.VMEM((n_heads, 1, tq), F32)]
        + [pltpu.VMEM((n_heads, tq, tq), F32)] * (1 if lookahead else 2),
        compiler_params=pltpu.CompilerParams(dimension_semantics=("parallel", "parallel"),
                                             vmem_limit_bytes=VMEM_LIMIT),
        name="attn_sel" if selected else ("attn_swa" if sinks is not None else "attn_win"),
    )(*args)


def _layer_norm(r, g, b):
    mu = jnp.mean(r, axis=-1, keepdims=True)
    d = r - mu
    var = jnp.mean(d * d, axis=-1, keepdims=True)
    return d * lax.rsqrt(var + LN_EPS) * g + b


def _mlp_kernel(x_ref, oa_ref, oc_ref, os_ref, ow_ref, gm_ref, p_ref, wa_ref, wb_ref, wo_ref, g1_ref, b1_ref,
                wg_ref, wu_ref, wd_ref, wpg_ref, wple_ref, g2_ref, b2_ref, o_ref, *, alpha):
    tm = x_ref.shape[0]
    blocks = [slice(r * tm // MLP_SPLIT, (r + 1) * tm // MLP_SPLIT) for r in range(MLP_SPLIT)]

    def merge(r):
        o_b = (oc_ref[r, :].astype(F32) + os_ref[r, :].astype(F32) + ow_ref[r, :].astype(F32)).astype(BF16)
        return (gm_ref[r, 0:D_MODEL].astype(F32) * _dot(oa_ref[r, :], wa_ref[...])
                + gm_ref[r, D_MODEL:2 * D_MODEL].astype(F32) * _dot(o_b, wb_ref[...]))

    def norm1(r, y):
        return _layer_norm(alpha * x_ref[r, :] + _dot(y.astype(BF16), wo_ref[...]), g1_ref[...], b1_ref[...])

    def swiglu(h):
        hb = h.astype(BF16)
        ff = jnp.zeros(h.shape, F32)
        for lo, hi in FF_CHUNKS:
            sl = slice(lo, hi)
            a = jax.nn.silu(_dot(hb, wg_ref[:, sl])) * _dot(hb, wu_ref[:, sl])
            ff = ff + _dot(a.astype(BF16), wd_ref[sl, :])
        return ff

    def norm2(r, h, ff):
        ple = jax.nn.sigmoid(_dot(h.astype(BF16), wpg_ref[...])) * _dot(p_ref[r, :].astype(BF16), wple_ref[...])
        o_ref[r, :] = _layer_norm(alpha * h + ff + ple, g2_ref[...], b2_ref[...])

    ys = [merge(r) for r in blocks]
    hs = [norm1(r, y) for r, y in zip(blocks, ys)]
    ffs = [swiglu(h) for h in hs]
    for r, h, ff in zip(blocks, hs, ffs):
        norm2(r, h, ff)


def _mlp(x2, o_a, o_c, o_s, o_w, gm, p2, w_a, w_b, w_o, ln1_g, ln1_b, w_g, w_u, w_d, w_pg, w_ple, ln2_g, ln2_b,
         alpha):
    T, D = x2.shape
    tm = MLP_TM
    row = lambda w: pl.BlockSpec((tm, w), lambda i: (i, 0))
    const = lambda shp: pl.BlockSpec(shp, lambda i: (0, 0), pipeline_mode=pl.Buffered(1))
    bf = lambda w: w.astype(BF16)
    return pl.pallas_call(
        functools.partial(_mlp_kernel, alpha=alpha),
        grid=(T // tm,),
        in_specs=[row(D), row(SWA_Q), row(NSA_Q), row(NSA_Q), row(NSA_Q), row(2 * D), row(PLE_DIM),
                  const((SWA_Q, D)), const((NSA_Q, D)), const((D, D)), const((1, D)), const((1, D)),
                  const((D, D_FF)), const((D, D_FF)), const((D_FF, D)), const((D, D)), const((PLE_DIM, D)),
                  const((1, D)), const((1, D))],
        out_specs=row(D),
        out_shape=jax.ShapeDtypeStruct((T, D), F32),
        compiler_params=pltpu.CompilerParams(dimension_semantics=("parallel",),
                                             vmem_limit_bytes=MLP_VMEM_LIMIT),
        name="mlp_ln",
    )(x2, o_a, o_c, o_s, o_w, gm, p2, bf(w_a), bf(w_b), bf(w_o), ln1_g[None, :], ln1_b[None, :],
      bf(w_g), bf(w_u), bf(w_d), bf(w_pg), bf(w_ple), ln2_g[None, :], ln2_b[None, :])


def kernel(x, p, positions, w_in, attn_sinks, cmp_pos_emb, w_cmp_k1, w_cmp_k2, w_cmp_v1, w_cmp_v2, w_proj_swa, w_proj_nsa, w_out, ln1_g, ln1_b, w_ff_gate, w_ff_up, w_ff_down, w_ple, w_ple_gate, ln2_g, ln2_b):
    B, S, D = x.shape
    depth = w_in.shape[0]
    assert D == D_MODEL and S % INPROJ_TM == 0 and S >= NSA_WINDOW + ATTN_TQ
    alpha = (2.0 * depth) ** 0.25
    h = x
    for i in range(depth):
        (ks, kw, ka, kc, vc, gm, qaT, qnrT, qnnT, vsT, vwT, vaT, gnT) = _inproj(h, positions, w_in[i])
        k_c, v_cT = _compress(kc, vc, cmp_pos_emb[i], w_cmp_k1[i], w_cmp_k2[i], w_cmp_v1[i], w_cmp_v2[i])
        o_cmp, selb = _cmp_select(qnnT, k_c, v_cT, gnT)
        o_slc = _attention(qnrT, ks, vsT, tq=SEL_TQ, rep=NSA_REP, n_sub=4, selected=True, gnT=gnT, gate_branch=1,
                           selb=selb)
        o_win = _attention(qnrT, kw, vwT, tq=ATTN_TQ, rep=NSA_REP, n_sub=4, nprev=NSA_WINDOW // ATTN_TQ, gnT=gnT,
                           gate_branch=2)
        o_swa = _attention(qaT, ka, vaT, tq=SWA_TQ, rep=SWA_HEADS, n_sub=SWA_SUBTILES,
                           nprev=SWA_WINDOW // SWA_TQ, sinks=attn_sinks[i].astype(F32), lookahead=0)
        flat = lambda a: a.reshape(B * S, a.shape[-1])
        h2 = _mlp(flat(h), flat(o_swa), flat(o_cmp), flat(o_slc), flat(o_win), flat(gm), flat(p[i]),
                  w_proj_swa[i], w_proj_nsa[i], w_out[i], ln1_g[i], ln1_b[i],
                  w_ff_gate[i], w_ff_up[i], w_ff_down[i], w_ple_gate[i], w_ple[i], ln2_g[i], ln2_b[i], alpha)
        h = h2.reshape(B, S, D)
    return h
```
